```python
import math
import jax
import jax.numpy as jnp
from jax import lax
import numpy as np

D_MODEL = 2048
BATCH = 2
SEQ = 16384
DEPTH = 1
DEC_BATCH = 16
DEC_SEQ = 32
PAST_LEN = 4096

CHUNK = 64
EPS = 1e-6
ATT_HEADS = 16
ATT_KV_HEADS = 4
ATT_GROUP = ATT_HEADS // ATT_KV_HEADS
HEAD_DIM = 64
ATT_WIDTH = ATT_HEADS * HEAD_DIM
KV_WIDTH = ATT_KV_HEADS * HEAD_DIM
WINDOW = 128
WIN_CHUNKS = WINDOW // CHUNK
ATT_SCALE = HEAD_DIM ** -0.5
SSD_HEADS = 16
SSD_HEADDIM = 64
SSD_WIDTH = SSD_HEADS * SSD_HEADDIM
SSD_GROUPS = 4
SSD_HPG = SSD_HEADS // SSD_GROUPS
SSD_STATE = 128
CONV_W = 4
CONV_DIM = SSD_WIDTH + 2 * SSD_GROUPS * SSD_STATE
MIX_WIDTH = ATT_WIDTH + SSD_WIDTH
Q_END = ATT_WIDTH
K_END = Q_END + KV_WIDTH
V_END = K_END + KV_WIDTH
Z_END = V_END + SSD_WIDTH
XBC_END = Z_END + CONV_DIM
IN_COLS = XBC_END + SSD_HEADS
IN_SPLITS = [Q_END, K_END, V_END, Z_END, XBC_END]
PEER_HEADS = 8
N_KEYS = 128
N_EXPERTS = N_KEYS * N_KEYS
D_KEY = 256
D_HALF = D_KEY // 2
PEER_TOPK = 16
PEER_BLOCK = 128

kernel_name = 'hymba_swa_sink_ssd_peer_stream_step'


def rmsnorm(x, g):
    xf = x.astype(jnp.float32)
    y = xf * lax.rsqrt(jnp.mean(xf * xf, axis=-1, keepdims=True) + EPS)
    return (y * g.astype(jnp.float32)).astype(x.dtype)


def alibi_slopes():
    return 2.0 ** (-8.0 * jnp.arange(1, ATT_HEADS + 1, dtype=jnp.float32) / ATT_HEADS)


def sink_softmax(logits, sink):
    m = jnp.maximum(jnp.max(logits, axis=-1, keepdims=True), sink)
    p = jnp.exp(logits - m)
    return p / (jnp.sum(p, axis=-1, keepdims=True) + jnp.exp(sink - m))


def window_attention_prompt(q, k, v, sinks):
    b, s = q.shape[0], q.shape[1]
    nc = s // CHUNK
    band_len = (WIN_CHUNKS + 1) * CHUNK
    qc = q.reshape(b, nc, CHUNK, ATT_KV_HEADS, ATT_GROUP, HEAD_DIM)

    def band(t):
        tc = t.reshape(b, nc, CHUNK, ATT_KV_HEADS, HEAD_DIM)
        tp = jnp.pad(tc, ((0, 0), (WIN_CHUNKS, 0), (0, 0), (0, 0), (0, 0)))
        return jnp.concatenate([tp[:, i:i + nc] for i in range(WIN_CHUNKS + 1)], axis=2)

    kb, vb = band(k), band(v)
    scores = jnp.einsum('bcqkgd,bcskd->bckgqs', qc, kb).astype(jnp.float32) * ATT_SCALE
    qi = jnp.arange(CHUNK)
    kj = jnp.arange(band_len)
    dist = jnp.abs(WIN_CHUNKS * CHUNK + qi[:, None] - kj[None, :]).astype(jnp.float32)
    bias = -alibi_slopes().reshape(ATT_KV_HEADS, ATT_GROUP, 1, 1) * dist
    key_chunk = jnp.arange(nc)[:, None] - WIN_CHUNKS + kj[None, :] // CHUNK
    valid = (key_chunk >= 0)[None, :, None, None, None, :]
    logits = jnp.where(valid, scores + bias, -jnp.inf)
    sink = sinks.astype(jnp.float32).reshape(1, 1, ATT_KV_HEADS, ATT_GROUP, 1, 1)
    probs = sink_softmax(logits, sink)
    out = jnp.einsum('bckgqs,bcskd->bcqkgd', probs.astype(vb.dtype), vb)
    return out.reshape(b, s, ATT_WIDTH)


def window_attention_sample(q, k_new, v_new, cache_k, cache_v, sinks):
    b, n = q.shape[0], q.shape[1]
    w = cache_k.shape[1]
    kf = jnp.concatenate([cache_k.astype(k_new.dtype), k_new], axis=1)
    vf = jnp.concatenate([cache_v.astype(v_new.dtype), v_new], axis=1)
    qg = q.reshape(b, n, ATT_KV_HEADS, ATT_GROUP, HEAD_DIM)
    scores = jnp.einsum('bqkgd,bskd->bkgqs', qg, kf).astype(jnp.float32) * ATT_SCALE
    dist = jnp.abs(jnp.arange(n)[:, None] - (jnp.arange(w + n)[None, :] - w)).astype(jnp.float32)
    bias = -alibi_slopes().reshape(ATT_KV_HEADS, ATT_GROUP, 1, 1) * dist
    sink = sinks.astype(jnp.float32).reshape(1, ATT_KV_HEADS, ATT_GROUP, 1, 1)
    probs = sink_softmax(scores + bias, sink)
    out = jnp.einsum('bkgqs,bskd->bqkgd', probs.astype(vf.dtype), vf)
    return out.reshape(b, n, ATT_WIDTH)


def causal_conv(xbc, prev, conv_w, conv_b):
    L = xbc.shape[1]
    xp = jnp.concatenate([prev.astype(xbc.dtype), xbc], axis=1)
    out = conv_b
    for i in range(CONV_W):
        out = out + xp[:, i:i + L] * conv_w[i]
    return jax.nn.silu(out), xp[:, L:]


def ssd_scan(xs, dt, a, bm, cm, h0, chunk):
    b, L = xs.shape[0], xs.shape[1]
    nc = L // chunk
    xr = xs.reshape(b, nc, chunk, SSD_GROUPS, SSD_HPG, SSD_HEADDIM)
    dtr = dt.reshape(b, nc, chunk, SSD_GROUPS, SSD_HPG)
    br = bm.reshape(b, nc, chunk, SSD_GROUPS, SSD_STATE)
    cr = cm.reshape(b, nc, chunk, SSD_GROUPS, SSD_STATE)
    acum = jnp.cumsum(dtr * a.reshape(SSD_GROUPS, SSD_HPG), axis=2)
    xdt = xr * dtr[..., None]
    li = jnp.arange(chunk)
    causal = (li[:, None] >= li[None, :])[:, :, None, None]
    seg = acum[:, :, :, None] - acum[:, :, None, :]
    decay = jnp.exp(jnp.where(causal, seg, -jnp.inf))
    cb = jnp.einsum('bclgn,bcsgn->bclsg', cr, br)
    y_diag = jnp.einsum('bclsg,bclsgr,bcsgrp->bclgrp', cb, decay, xdt)
    decay_end = jnp.exp(acum[:, :, -1:] - acum)
    states = jnp.einsum('bclgn,bclgr,bclgrp->bcgrpn', br, decay_end, xdt)
    chunk_decay = jnp.exp(acum[:, :, -1])

    def step(h, inp):
        st, dc = inp
        return h * dc[..., None, None] + st, h

    h_init = h0.reshape(b, SSD_GROUPS, SSD_HPG, SSD_HEADDIM, SSD_STATE)
    h_last, h_prev = lax.scan(step, h_init, (jnp.moveaxis(states, 1, 0), jnp.moveaxis(chunk_decay, 1, 0)))
    h_prev = jnp.moveaxis(h_prev, 0, 1)
    y_off = jnp.einsum('bclgn,bcgrpn,bclgr->bclgrp', cr, h_prev, jnp.exp(acum))
    y = (y_diag + y_off).reshape(b, L, SSD_HEADS, SSD_HEADDIM)
    return y, h_last.reshape(b, SSD_HEADS, SSD_HEADDIM, SSD_STATE)


def ssd_mixer(z, xbc, dt_raw, conv_prev, h0, conv_w, conv_b, dt_bias, a_log, d_skip, g_ssd, chunk):
    b, L = z.shape[0], z.shape[1]
    xbc_act, conv_state = causal_conv(xbc, conv_prev, conv_w, conv_b)
    xs, bm, cm = jnp.split(xbc_act.astype(jnp.float32), [SSD_WIDTH, SSD_WIDTH + SSD_GROUPS * SSD_STATE], axis=-1)
    xs = xs.reshape(b, L, SSD_HEADS, SSD_HEADDIM)
    bm = bm.reshape(b, L, SSD_GROUPS, SSD_STATE)
    cm = cm.reshape(b, L, SSD_GROUPS, SSD_STATE)
    dt = jax.nn.softplus(dt_raw.astype(jnp.float32) + dt_bias.astype(jnp.float32))
    a = -jnp.exp(a_log.astype(jnp.float32))
    y, h = ssd_scan(xs, dt, a, bm, cm, h0.astype(jnp.float32), chunk)
    y = y + d_skip.astype(jnp.float32)[:, None] * xs
    y = y.reshape(b, L, SSD_WIDTH) * jax.nn.silu(z.astype(jnp.float32))
    return rmsnorm(y, g_ssd), conv_state, h


def peer_ffn(h, w_peer_q, peer_sub_keys, peer_u, peer_v):
    b, L, d = h.shape
    t = b * L
    nblk = -(-t // PEER_BLOCK)
    hf = jnp.pad(h.reshape(t, d), ((0, nblk * PEER_BLOCK - t), (0, 0)))
    sub_keys = peer_sub_keys.astype(jnp.float32)

    def block(hb):
        q = (hb @ w_peer_q).astype(jnp.float32).reshape(PEER_BLOCK, PEER_HEADS, 2, D_HALF)
        s = jnp.einsum('thkd,hknd->thkn', q, sub_keys)
        v1, i1 = lax.top_k(s[:, :, 0], PEER_TOPK)
        v2, i2 = lax.top_k(s[:, :, 1], PEER_TOPK)
        cand = (v1[..., :, None] + v2[..., None, :]).reshape(PEER_BLOCK, PEER_HEADS, PEER_TOPK * PEER_TOPK)
        top, pos = lax.top_k(cand, PEER_TOPK)
        e1 = jnp.take_along_axis(i1, pos // PEER_TOPK, axis=-1)
        e2 = jnp.take_along_axis(i2, pos % PEER_TOPK, axis=-1)
        idx = e1 * N_KEYS + e2
        gate = jax.nn.softmax(top, axis=-1)
        u = jnp.take(peer_u, idx, axis=0)
        act = jax.nn.gelu(jnp.einsum('thkd,td->thk', u, hb).astype(jnp.float32), approximate=False)
        vv = jnp.take(peer_v, idx, axis=0)
        return jnp.einsum('thk,thkd->td', (gate * act).astype(vv.dtype), vv)

    out = lax.map(block, hf.reshape(nblk, PEER_BLOCK, d))
    return out.reshape(nblk * PEER_BLOCK, d)[:t].reshape(b, L, d).astype(h.dtype)


def hybrid_layer(x, c, past, w_ada, b_ada, g_mix, w_in, attn_sinks, g_attn_out, conv_w, conv_b,
                 dt_bias, a_log, d_skip, g_ssd, w_out, g_ffn, w_peer_q, peer_sub_keys, peer_u, peer_v):
    b, L = x.shape[0], x.shape[1]
    mod = jax.nn.silu(c) @ w_ada + b_ada
    sh1, sc1, gt1, sh2, sc2, gt2 = [m[:, None, :] for m in jnp.split(mod, 6, axis=-1)]
    h = rmsnorm(x, g_mix) * (1 + sc1) + sh1
    q, k, v, z, xbc, dt_raw = jnp.split(h @ w_in, IN_SPLITS, axis=-1)
    q = q.reshape(b, L, ATT_HEADS, HEAD_DIM)
    k = k.reshape(b, L, ATT_KV_HEADS, HEAD_DIM)
    v = v.reshape(b, L, ATT_KV_HEADS, HEAD_DIM)
    if past is None:
        att = window_attention_prompt(q, k, v, attn_sinks)
        conv_prev = jnp.zeros((b, CONV_W - 1, CONV_DIM), x.dtype)
        h0 = jnp.zeros((b, SSD_HEADS, SSD_HEADDIM, SSD_STATE), jnp.float32)
        scan_chunk = CHUNK
        new_k, new_v = k[:, -WINDOW:], v[:, -WINDOW:]
    else:
        cache_k, cache_v, conv_prev, h0 = past
        att = window_attention_sample(q, k, v, cache_k, cache_v, attn_sinks)
        scan_chunk = L
        new_k, new_v = k, v
    y_ssd, conv_state, ssm_state = ssd_mixer(z, xbc, dt_raw, conv_prev, h0, conv_w, conv_b,
                                             dt_bias, a_log, d_skip, g_ssd, scan_chunk)
    mixed = jnp.concatenate([rmsnorm(att, g_attn_out), y_ssd.astype(x.dtype)], axis=-1) @ w_out
    x = x + gt1 * mixed
    h2 = rmsnorm(x, g_ffn) * (1 + sc2) + sh2
    x = x + gt2 * peer_ffn(h2, w_peer_q, peer_sub_keys, peer_u, peer_v)
    return x, (new_k, new_v, conv_state, ssm_state.astype(x.dtype))


def setup_inputs(seed: int = 0) -> dict:
    key = jax.random.key(seed)
    ks = jax.random.split(key, 32)
    f32 = jnp.float32
    D = D_MODEL
    cache_rows = min(WINDOW, PAST_LEN)

    def nrm(k, shape, s):
        return jax.random.normal(k, shape, f32) * s

    dt0 = jnp.exp(jax.random.uniform(ks[16], (DEPTH, SSD_HEADS), f32, math.log(1e-3), math.log(1e-1)))
    return {
        'x_prompt': nrm(ks[0], (BATCH, SEQ, D), 1.0),
        'x_sample': nrm(ks[1], (DEC_BATCH, DEC_SEQ, D), 1.0),
        'c_prompt': nrm(ks[2], (BATCH, D), 1.0),
        'c_sample': nrm(ks[3], (DEC_BATCH, D), 1.0),
        'cache_k': nrm(ks[4], (DEPTH, DEC_BATCH, cache_rows, ATT_KV_HEADS, HEAD_DIM), 1.0),
        'cache_v': nrm(ks[5], (DEPTH, DEC_BATCH, cache_rows, ATT_KV_HEADS, HEAD_DIM), 1.0),
        'state_conv': nrm(ks[6], (DEPTH, DEC_BATCH, CONV_W - 1, CONV_DIM), 1.0),
        'state_ssm': nrm(ks[7], (DEPTH, DEC_BATCH, SSD_HEADS, SSD_HEADDIM, SSD_STATE), 0.5),
        'w_ada': nrm(ks[8], (DEPTH, D, 6 * D), 0.5 * D ** -0.5),
        'b_ada': nrm(ks[9], (DEPTH, 6 * D), 0.01),
        'g_mix': 1.0 + nrm(ks[10], (DEPTH, D), 0.01),
        'w_in': nrm(ks[11], (DEPTH, D, IN_COLS), D ** -0.5),
        'attn_sinks': nrm(ks[12], (DEPTH, ATT_HEADS), 0.5),
        'g_attn_out': 1.0 + nrm(ks[13], (DEPTH, ATT_WIDTH), 0.01),
        'conv_w': nrm(ks[14], (DEPTH, CONV_W, CONV_DIM), CONV_W ** -0.5),
        'conv_b': nrm(ks[15], (DEPTH, CONV_DIM), 0.01),
        'dt_bias': dt0 + jnp.log(-jnp.expm1(-dt0)),
        'a_log': jnp.log(jax.random.uniform(ks[17], (DEPTH, SSD_HEADS), f32, 1.0, 16.0)),
        'd_skip': 1.0 + nrm(ks[18], (DEPTH, SSD_HEADS), 0.1),
        'g_ssd': 1.0 + nrm(ks[19], (DEPTH, SSD_WIDTH), 0.01),
        'w_out': nrm(ks[20], (DEPTH, MIX_WIDTH, D), MIX_WIDTH ** -0.5),
        'g_ffn': 1.0 + nrm(ks[21], (DEPTH, D), 0.01),
        'w_peer_q': nrm(ks[22], (DEPTH, D, PEER_HEADS * D_KEY), D ** -0.5),
        'peer_sub_keys': nrm(ks[23], (DEPTH, PEER_HEADS, 2, N_KEYS, D_HALF), D_HALF ** -0.5),
        'peer_u': nrm(ks[24], (DEPTH, N_EXPERTS, D), D ** -0.5),
        'peer_v': nrm(ks[25], (DEPTH, N_EXPERTS, D), 0.5),
        'g_final': 1.0 + nrm(ks[26], (D,), 0.01),
    }


def reference(x_prompt, x_sample, c_prompt, c_sample, cache_k, cache_v, state_conv, state_ssm,
              w_ada, b_ada, g_mix, w_in, attn_sinks, g_attn_out, conv_w, conv_b, dt_bias, a_log,
              d_skip, g_ssd, w_out, g_ffn, w_peer_q, peer_sub_keys, peer_u, peer_v, g_final):
    hp, hs = x_prompt, x_sample
    kp, vp, cp, sp = [], [], [], []
    ksm, vsm, csm, ssm = [], [], [], []
    for l in range(DEPTH):
        lw = (w_ada[l], b_ada[l], g_mix[l], w_in[l], attn_sinks[l], g_attn_out[l], conv_w[l], conv_b[l],
              dt_bias[l], a_log[l], d_skip[l], g_ssd[l], w_out[l], g_ffn[l], w_peer_q[l],
              peer_sub_keys[l], peer_u[l], peer_v[l])
        hp, (k_p, v_p, c_p, s_p) = hybrid_layer(hp, c_prompt, None, *lw)
        hs, (k_s, v_s, c_s, s_s) = hybrid_layer(
            hs, c_sample, (cache_k[l], cache_v[l], state_conv[l], state_ssm[l]), *lw)
        kp.append(k_p)
        vp.append(v_p)
        cp.append(c_p)
        sp.append(s_p)
        ksm.append(k_s)
        vsm.append(v_s)
        csm.append(c_s)
        ssm.append(s_s)
    y_prompt = rmsnorm(hp, g_final)
    y_sample = rmsnorm(hs, g_final)
    return (y_prompt, y_sample, jnp.stack(kp), jnp.stack(vp), jnp.stack(cp), jnp.stack(sp),
            jnp.stack(ksm), jnp.stack(vsm), jnp.stack(csm), jnp.stack(ssm))
```

```python
import functools
import math

import jax
import jax.numpy as jnp
from jax import lax
from jax.experimental import pallas as pl
from jax.experimental.pallas import tpu as pltpu

F32 = jnp.float32
BF16 = jnp.bfloat16

EPS = 1e-6
CHUNK = 64
WINDOW = 128
ATT_HEADS = 16
ATT_KV_HEADS = 4
ATT_GROUP = ATT_HEADS // ATT_KV_HEADS
HEAD_DIM = 64
ATT_WIDTH = ATT_HEADS * HEAD_DIM
KV_WIDTH = ATT_KV_HEADS * HEAD_DIM
ATT_SCALE = HEAD_DIM ** -0.5
SSD_HEADS = 16
SSD_HEADDIM = 64
SSD_WIDTH = SSD_HEADS * SSD_HEADDIM
SSD_GROUPS = 4
SSD_HPG = SSD_HEADS // SSD_GROUPS
SSD_STATE = 128
CONV_W = 4
CONV_DIM = SSD_WIDTH + 2 * SSD_GROUPS * SSD_STATE
PEER_HEADS = 8
N_KEYS = 128
D_HALF = 128
PEER_TOPK = 16

LANES = 128
SUBLANES = 8
VMEM_LIMIT = 56 * 1024 * 1024
INV_SQRT2 = 0.7071067811865476
NEG_INF = float("-inf")
POS_INF = float("inf")


def _params(sem):
    return pltpu.CompilerParams(dimension_semantics=sem, vmem_limit_bytes=VMEM_LIMIT)


def _resident(shape):
    nd = len(shape)
    return pl.BlockSpec(shape, lambda *_: (0,) * nd, pipeline_mode=pl.Buffered(1))


def _rms(x, g):
    return x * lax.rsqrt(jnp.mean(x * x, axis=-1, keepdims=True) + EPS) * g


def _silu(x):
    return x * jax.nn.sigmoid(x)


def _softplus(x):
    return jnp.maximum(x, 0.0) + jnp.log1p(jnp.exp(-jnp.abs(x)))


def _dot(a, b):
    return jnp.dot(a, b, preferred_element_type=F32)


def _dot_nt(a, b):
    return lax.dot_general(a, b, (((1,), (1,)), ((), ())), preferred_element_type=F32)


def _mod_kernel(c_ref, w_ref, b_ref, o_ref):
    s = _silu(c_ref[...]).astype(BF16)
    o_ref[...] = _dot(s, w_ref[...].astype(BF16)) + b_ref[...]


def _mod_call(c_all, w_ada, b_ada):
    rows, d = c_all.shape
    n = w_ada.shape[1]
    bn = 1536
    assert n % bn == 0
    return pl.pallas_call(
        _mod_kernel,
        grid=(n // bn,),
        in_specs=[pl.BlockSpec((rows, d), lambda j: (0, 0)),
                  pl.BlockSpec((d, bn), lambda j: (0, j)),
                  pl.BlockSpec((1, bn), lambda j: (0, j))],
        out_specs=pl.BlockSpec((rows, bn), lambda j: (0, j)),
        out_shape=jax.ShapeDtypeStruct((rows, n), F32),
        compiler_params=_params(("arbitrary",)),
        name="mod",
    )(c_all, w_ada, b_ada.reshape(1, n))


def _inproj_kernel(x_ref, sh_ref, sc_ref, g_ref, wq, wk, wv, wz, wx, wdt,
                   q_o, k_o, v_o, z_o, xbc_o, dt_o):
    h = _rms(x_ref[...], g_ref[...]) * (1.0 + sc_ref[...]) + sh_ref[...]
    hb = h.astype(BF16)
    q_o[...] = _dot(hb, wq[...]).astype(q_o.dtype)
    k_o[...] = _dot(hb, wk[...])
    v_o[...] = _dot(hb, wv[...])
    z_o[...] = _dot(hb, wz[...])
    xbc_o[...] = _dot(hb, wx[...])
    dt_o[...] = _dot(hb, wdt[...])


def _mod_spec(arr, tm):
    r = arr.shape[1]
    d = arr.shape[2]
    if r == 1:
        return pl.BlockSpec((None, 1, d), lambda b, i: (b, 0, 0))
    return pl.BlockSpec((None, tm, d), lambda b, i: (b, i, 0))


def _inproj_call(x, sh, sc, g_mix, ws, tm):
    G, L, D = x.shape
    wq, wk, wv, wz, wx, wdt = ws
    tok = lambda n: pl.BlockSpec((None, tm, n), lambda b, i: (b, i, 0))
    out_w = (ATT_WIDTH, KV_WIDTH, KV_WIDTH, SSD_WIDTH, CONV_DIM, LANES)
    out_dt = (BF16, F32, F32, F32, F32, F32)
    return pl.pallas_call(
        _inproj_kernel,
        grid=(G, L // tm),
        in_specs=[tok(D), _mod_spec(sh, tm), _mod_spec(sc, tm), _resident((1, D))]
                 + [_resident(w.shape) for w in ws],
        out_specs=[tok(n) for n in out_w],
        out_shape=[jax.ShapeDtypeStruct((G, L, n), dt) for n, dt in zip(out_w, out_dt)],
        compiler_params=_params(("parallel", "arbitrary")),
        name="inproj",
    )(x, sh, sc, g_mix, *ws)


def _alibi_slope(h):
    return 2.0 ** (-8.0 * (h + 1) / ATT_HEADS)


def _attn_kernel(sink_ref, q_ref, kp_ref, kc_ref, vp_ref, vc_ref, g_ref, o_ref, *, banded, qb):
    blk = pl.program_id(1)
    nk = WINDOW + qb
    r = lax.broadcasted_iota(jnp.int32, (qb, nk), 0)
    j = lax.broadcasted_iota(jnp.int32, (qb, nk), 1)
    dist = jnp.abs(r + WINDOW - j).astype(F32)
    if banded:
        qc = jnp.right_shift(r, 6)
        kc = jnp.right_shift(j, 6) - WINDOW // CHUNK
        valid = (kc <= qc) & (kc >= qc - WINDOW // CHUNK) & ((j >= WINDOW) | (blk > 0))
    q = q_ref[...]
    outs = []
    for kv in range(ATT_KV_HEADS):
        sl = slice(kv * HEAD_DIM, (kv + 1) * HEAD_DIM)
        kb = jnp.concatenate([kp_ref[:, sl], kc_ref[:, sl]], axis=0).astype(BF16)
        vb = jnp.concatenate([vp_ref[:, sl], vc_ref[:, sl]], axis=0).astype(BF16)
        for g in range(ATT_GROUP):
            h = kv * ATT_GROUP + g
            qh = q[:, h * HEAD_DIM:(h + 1) * HEAD_DIM]
            logits = _dot_nt(qh, kb) * ATT_SCALE - _alibi_slope(h) * dist
            if banded:
                logits = jnp.where(valid, logits, NEG_INF)
            sink = sink_ref[h]
            m = jnp.maximum(jnp.max(logits, axis=-1, keepdims=True), sink)
            p = jnp.exp(logits - m)
            den = jnp.sum(p, axis=-1, keepdims=True) + jnp.exp(sink - m)
            outs.append(_dot(p.astype(BF16), vb) / den)
    att = jnp.concatenate(outs, axis=-1)
    o_ref[...] = _rms(att, g_ref[...]).astype(o_ref.dtype)


def _attn_call(q, k, v, k_prev, v_prev, sinks, g_att, qb, banded):
    B, L, _ = q.shape
    nb = L // qb
    wpb = qb // WINDOW
    cur = lambda n: pl.BlockSpec((None, qb, n), lambda b, i: (b, i, 0))
    if k_prev is None:
        prev = pl.BlockSpec((None, WINDOW, KV_WIDTH), lambda b, i: (b, jnp.maximum(i * wpb - 1, 0), 0))
        k_prev, v_prev = k, v
    else:
        prev = pl.BlockSpec((None, WINDOW, KV_WIDTH), lambda b, i: (b, 0, 0))
    return pl.pallas_call(
        functools.partial(_attn_kernel, banded=banded, qb=qb),
        grid=(B, nb),
        in_specs=[pl.BlockSpec(memory_space=pltpu.SMEM),
                  cur(ATT_WIDTH), prev, cur(KV_WIDTH), prev, cur(KV_WIDTH), _resident((1, ATT_WIDTH))],
        out_specs=cur(ATT_WIDTH),
        out_shape=jax.ShapeDtypeStruct((B, L, ATT_WIDTH), BF16),
        compiler_params=_params(("parallel", "arbitrary")),
        name="attn",
    )(sinks, q, k_prev, k, v_prev, v, g_att)


def _ssd_kernel(dskip_ref, xbc_ref, z_ref, dt_ref, tail0_ref, h0_ref, cw_ref, cb_ref, dtb_ref, alog_ref, g_ref,
                y_ref, tail_ref, hT_ref, xext, hT, *, lc):
    i = pl.program_id(1)

    @pl.when(i == 0)
    def _():
        xext[0:SUBLANES, :] = tail0_ref[...]
        hT[...] = h0_ref[...]

    xext[SUBLANES:SUBLANES + lc, :] = xbc_ref[...]
    acc = cb_ref[...]
    for t in range(CONV_W):
        off = SUBLANES - (CONV_W - 1) + t
        acc = acc + xext[off:off + lc, :] * cw_ref[t:t + 1, :]
    xa = _silu(acc)
    tail = xext[lc:lc + SUBLANES, :]
    xext[0:SUBLANES, :] = tail
    tail_ref[...] = tail

    xs = xa[:, :SSD_WIDTH]
    bm = xa[:, SSD_WIDTH:SSD_WIDTH + SSD_GROUPS * SSD_STATE]
    cm = xa[:, SSD_WIDTH + SSD_GROUPS * SSD_STATE:]
    dt = _softplus(dt_ref[...] + dtb_ref[...])
    da = dt * (-jnp.exp(alog_ref[...]))
    row = lax.broadcasted_iota(jnp.int32, (lc, lc), 0)
    col = lax.broadcasted_iota(jnp.int32, (lc, lc), 1)
    causal = row >= col
    acum = jnp.dot(causal.astype(F32), da, precision=lax.Precision.HIGHEST, preferred_element_type=F32)
    acum_t = acum.T
    a_last = acum[lc - 1:lc, :]
    dec_end = jnp.exp(a_last - acum)
    eac = jnp.exp(acum)
    cdec = jnp.exp(a_last)
    ys = []
    for g in range(SSD_GROUPS):
        bg = bm[:, g * SSD_STATE:(g + 1) * SSD_STATE]
        cgb = cm[:, g * SSD_STATE:(g + 1) * SSD_STATE].astype(BF16)
        cb = _dot_nt(cgb, bg.astype(BF16))
        bgt = bg.T.astype(BF16)
        for r in range(SSD_HPG):
            h = g * SSD_HPG + r
            seg = acum[:, h:h + 1] - acum_t[h:h + 1, :]
            decay = jnp.exp(jnp.where(causal, seg, NEG_INF))
            xh = xs[:, h * SSD_HEADDIM:(h + 1) * SSD_HEADDIM]
            xdt = xh * dt[:, h:h + 1]
            hprev = hT[h]
            y = _dot((cb * decay).astype(BF16), xdt.astype(BF16))
            y = y + _dot(cgb, hprev.astype(BF16)) * eac[:, h:h + 1]
            st = _dot(bgt, (xdt * dec_end[:, h:h + 1]).astype(BF16))
            hT[h] = hprev * cdec[:, h:h + 1] + st
            ys.append(y + dskip_ref[h] * xh)
    y = jnp.concatenate(ys, axis=-1) * _silu(z_ref[...])
    y_ref[...] = _rms(y, g_ref[...]).astype(y_ref.dtype)
    hT_ref[...] = hT[...]


def _ssd_call(xbc, z, dt, tail0, h0t, conv_w, conv_b, dt_bias, a_log, d_skip, g_ssd, lc):
    B, L, _ = xbc.shape
    tok = lambda n: pl.BlockSpec((None, lc, n), lambda b, i: (b, i, 0))
    per_b = lambda shp: pl.BlockSpec((None,) + shp, lambda b, i: (b,) + (0,) * len(shp))
    return pl.pallas_call(
        functools.partial(_ssd_kernel, lc=lc),
        grid=(B, L // lc),
        in_specs=[pl.BlockSpec(memory_space=pltpu.SMEM),
                  tok(CONV_DIM), tok(SSD_WIDTH), tok(LANES),
                  per_b((SUBLANES, CONV_DIM)), per_b((SSD_HEADS, SSD_STATE, SSD_HEADDIM)),
                  _resident((CONV_W, CONV_DIM)), _resident((1, CONV_DIM)),
                  _resident((1, LANES)), _resident((1, LANES)), _resident((1, SSD_WIDTH))],
        out_specs=[tok(SSD_WIDTH), per_b((SUBLANES, CONV_DIM)), per_b((SSD_HEADS, SSD_STATE, SSD_HEADDIM))],
        out_shape=[jax.ShapeDtypeStruct((B, L, SSD_WIDTH), BF16),
                   jax.ShapeDtypeStruct((B, SUBLANES, CONV_DIM), F32),
                   jax.ShapeDtypeStruct((B, SSD_HEADS, SSD_STATE, SSD_HEADDIM), F32)],
        scratch_shapes=[pltpu.VMEM((lc + SUBLANES, CONV_DIM), F32),
                        pltpu.VMEM((SSD_HEADS, SSD_STATE, SSD_HEADDIM), F32)],
        compiler_params=_params(("parallel", "arbitrary")),
        name="ssd",
    )(d_skip, xbc, z, dt, tail0, h0t, conv_w, conv_b, dt_bias, a_log, g_ssd)


def _outproj_kernel(att_ref, y_ref, x_ref, gt_ref, sh_ref, sc_ref, g_ref, wa, wy, x1_o, h2_o):
    mixed = _dot(att_ref[...], wa[...]) + _dot(y_ref[...], wy[...])
    x1 = x_ref[...] + gt_ref[...] * mixed
    x1_o[...] = x1
    h2_o[...] = (_rms(x1, g_ref[...]) * (1.0 + sc_ref[...]) + sh_ref[...]).astype(h2_o.dtype)


def _outproj_call(att, y, x, gt, sh, sc, g_ffn, wa, wy, tm):
    G, L, D = x.shape
    tok = lambda n: pl.BlockSpec((None, tm, n), lambda b, i: (b, i, 0))
    return pl.pallas_call(
        _outproj_kernel,
        grid=(G, L // tm),
        in_specs=[tok(ATT_WIDTH), tok(SSD_WIDTH), tok(D), _mod_spec(gt, tm), _mod_spec(sh, tm), _mod_spec(sc, tm),
                  _resident((1, D)), _resident(wa.shape), _resident(wy.shape)],
        out_specs=[tok(D), tok(D)],
        out_shape=[jax.ShapeDtypeStruct((G, L, D), F32), jax.ShapeDtypeStruct((G, L, D), BF16)],
        compiler_params=_params(("parallel", "arbitrary")),
        name="outproj",
    )(att, y, x, gt, sh, sc, g_ffn, wa, wy)


def _top_rows(s, k):
    rows = []
    cur = s
    for _ in range(k):
        m = jnp.max(cur, axis=0, keepdims=True)
        rows.append(m)
        cur = jnp.where(cur == m, NEG_INF, cur)
    return rows


def _pscore_kernel(h2_ref, wqt_ref, sk_ref, a_o, th_o, b_o, s2_o):
    qt = _dot_nt(wqt_ref[...], h2_ref[...])
    ts = qt.shape[1]
    sub = lax.broadcasted_iota(jnp.int32, (PEER_TOPK, ts), 0)
    for h in range(PEER_HEADS):
        base = h * 2 * D_HALF
        s1 = _dot(sk_ref[h, 0], qt[base:base + D_HALF].astype(BF16))
        s2 = _dot(sk_ref[h, 1], qt[base + D_HALF:base + 2 * D_HALF].astype(BF16))
        v1 = _top_rows(s1, PEER_TOPK)
        v2 = _top_rows(s2, PEER_TOPK)
        v1m = jnp.concatenate(v1, axis=0)
        cands = [jnp.where(sub < PEER_TOPK // (jj + 1), v1m + v2[jj], NEG_INF) for jj in range(PEER_TOPK)]
        cur = cands
        tau = None
        for _ in range(PEER_TOPK):
            m = cur[0]
            for c in cur[1:]:
                m = jnp.maximum(m, c)
            tau = jnp.max(m, axis=0, keepdims=True)
            cur = [jnp.where(c == tau, NEG_INF, c) for c in cur]
        a1 = jnp.exp(v1m - v1[0])
        zsum = jnp.zeros_like(a1)
        for jj in range(PEER_TOPK):
            zsum = zsum + jnp.where(cands[jj] >= tau, a1 * jnp.exp(v2[jj] - v2[0]), 0.0)
        z = jnp.sum(zsum, axis=0, keepdims=True)
        th = jnp.full(s1.shape, POS_INF, F32)
        for jj in range(PEER_TOPK):
            th = jnp.minimum(th, jnp.where(s1 + v2[jj] >= tau, v2[jj], POS_INF))
        a_o[h] = jnp.exp(s1 - v1[0]) / z
        th_o[h] = th
        b_o[h] = jnp.exp(s2 - v2[0])
        s2_o[h] = s2


def _pscore_call(h2, wqt, sk, ts):
    T, D = h2.shape
    out = jax.ShapeDtypeStruct((PEER_HEADS, N_KEYS, T), F32)
    ospec = pl.BlockSpec((PEER_HEADS, N_KEYS, ts), lambda i: (0, 0, i))
    return pl.pallas_call(
        _pscore_kernel,
        grid=(T // ts,),
        in_specs=[pl.BlockSpec((ts, D), lambda i: (i, 0)), _resident(wqt.shape), _resident(sk.shape)],
        out_specs=[ospec] * 4,
        out_shape=[out] * 4,
        compiler_params=_params(("parallel",)),
        name="pscore",
    )(h2, wqt, sk)


def _pdense_kernel(h2_ref, u_ref, vt_ref, a_ref, th_ref, b_ref, s2_ref, x1_ref, gt_ref, gf_ref,
                   y_ref, h2t, acc, w_scr, *, n1b):
    e = pl.program_id(1)

    @pl.when(e == 0)
    def _():
        h2t[...] = h2_ref[...].astype(F32).T.astype(BF16)
        acc[...] = jnp.zeros_like(acc)

    act = _dot(u_ref[...], h2t[...])
    gel = 0.5 * act * (1.0 + lax.erf(act * INV_SQRT2))
    for k in range(n1b):
        g = None
        for h in range(PEER_HEADS):
            c = jnp.where(s2_ref[h] >= th_ref[h, k:k + 1, :], a_ref[h, k:k + 1, :] * b_ref[h], 0.0)
            g = c if g is None else g + c
        w_scr[k * N_KEYS:(k + 1) * N_KEYS, :] = (gel[k * N_KEYS:(k + 1) * N_KEYS] * g).astype(BF16)
    acc[...] += _dot(vt_ref[...], w_scr[...])

    @pl.when(e == pl.num_programs(1) - 1)
    def _():
        y = x1_ref[...] + gt_ref[...] * acc[...].T
        y_ref[...] = _rms(y, gf_ref[...])


def _pdense_call(h2, u, vt, a, th, b, s2, x1, gt, g_final, tt, n1b):
    T, D = h2.shape
    ne = u.shape[0]
    eb = n1b * N_KEYS
    tpg = T // tt // gt.shape[0]
    if gt.shape[1] == 1:
        gt_spec = pl.BlockSpec((None, 1, D), lambda i, e: (i // tpg, 0, 0))
    else:
        gt_spec = pl.BlockSpec((None, tt, D), lambda i, e: (i // tpg, i % tpg, 0))
    a = a.reshape(PEER_HEADS, N_KEYS // n1b, n1b, T)
    th = th.reshape(PEER_HEADS, N_KEYS // n1b, n1b, T)
    sel = pl.BlockSpec((PEER_HEADS, None, n1b, tt), lambda i, e: (0, e, 0, i))
    full = pl.BlockSpec((PEER_HEADS, N_KEYS, tt), lambda i, e: (0, 0, i))
    tok = pl.BlockSpec((tt, D), lambda i, e: (i, 0))
    return pl.pallas_call(
        functools.partial(_pdense_kernel, n1b=n1b),
        grid=(T // tt, ne // eb),
        in_specs=[tok, pl.BlockSpec((eb, D), lambda i, e: (e, 0)), pl.BlockSpec((D, eb), lambda i, e: (0, e)),
                  sel, sel, full, full, tok, gt_spec, _resident((1, D))],
        out_specs=tok,
        out_shape=jax.ShapeDtypeStruct((T, D), F32),
        scratch_shapes=[pltpu.VMEM((D, tt), BF16), pltpu.VMEM((D, tt), F32), pltpu.VMEM((eb, tt), BF16)],
        compiler_params=_params(("parallel", "arbitrary")),
        name="pdense",
    )(h2, u, vt, a, th, b, s2, x1, gt, g_final)


def _tile(n, pref):
    t = min(n, pref)
    assert n % t == 0, (n, pref)
    return t


def _layer(x, mods, past, lw, g_final, batch, final):
    G, L, D = x.shape
    T = G * L
    sh1, sc1, gt1, sh2, sc2, gt2 = mods
    tm = _tile(L, 256)
    q, k, v, z, xbc, dt = _inproj_call(x, sh1, sc1, lw["g_mix"], lw["w_in"], tm)
    seq = T // batch
    bs = lambda t: t.reshape(batch, seq, t.shape[-1])
    q, k, v, z, xbc, dt = [bs(t) for t in (q, k, v, z, xbc, dt)]
    if past is None:
        att = _attn_call(q, k, v, None, None, lw["sinks"], lw["g_att"], _tile(seq, 256), True)
        tail0 = jnp.zeros((batch, SUBLANES, CONV_DIM), F32)
        h0t = jnp.zeros((batch, SSD_HEADS, SSD_STATE, SSD_HEADDIM), F32)
        lc = _tile(seq, 128)
        new_k, new_v = k[:, -WINDOW:], v[:, -WINDOW:]
    else:
        cache_k, cache_v, conv_prev, h0 = past
        att = _attn_call(q, k, v, cache_k.reshape(batch, WINDOW, KV_WIDTH), cache_v.reshape(batch, WINDOW, KV_WIDTH),
                         lw["sinks"], lw["g_att"], seq, False)
        tail0 = jnp.pad(conv_prev, ((0, 0), (SUBLANES - (CONV_W - 1), 0), (0, 0)))
        h0t = jnp.swapaxes(h0, -1, -2)
        lc = seq
        new_k, new_v = k, v
    y_ssd, tail, ht = _ssd_call(xbc, z, dt, tail0, h0t, lw["conv_w"], lw["conv_b"], lw["dt_bias"], lw["a_log"],
                                lw["d_skip"], lw["g_ssd"], lc)
    gs = lambda t: t.reshape(G, L, t.shape[-1])
    x1, h2 = _outproj_call(gs(att), gs(y_ssd), x, gt1, sh2, sc2, lw["g_ffn"], lw["w_out_a"], lw["w_out_y"], tm)
    h2f = h2.reshape(T, D)
    ts = _tile(T, 256)
    a, th, b, s2 = _pscore_call(h2f, lw["wq_t"], lw["sub_keys"], ts)
    tt = _tile(T, 512)
    gf = g_final if final else jnp.ones_like(g_final)
    y = _pdense_call(h2f, lw["peer_u"], lw["peer_vt"], a, th, b, s2, x1.reshape(T, D), gt2, gf, tt, 4)
    assert final
    new_k = new_k.reshape(batch, -1, ATT_KV_HEADS, HEAD_DIM)
    new_v = new_v.reshape(batch, -1, ATT_KV_HEADS, HEAD_DIM)
    conv_state = tail[:, SUBLANES - (CONV_W - 1):, :]
    ssm = jnp.swapaxes(ht, -1, -2)
    return y.reshape(G, L, D), (new_k, new_v, conv_state, ssm)


def _layer_weights(l, w_in, g_mix, attn_sinks, g_attn_out, conv_w, conv_b, dt_bias, a_log, d_skip, g_ssd, w_out,
                   g_ffn, w_peer_q, peer_sub_keys, peer_u, peer_v):
    d = w_in.shape[1]
    wb = w_in[l].astype(BF16)
    q_end = ATT_WIDTH
    k_end = q_end + KV_WIDTH
    v_end = k_end + KV_WIDTH
    z_end = v_end + SSD_WIDTH
    x_end = z_end + CONV_DIM
    wdt = jnp.pad(wb[:, x_end:], ((0, 0), (0, LANES - SSD_HEADS)))
    pad_h = lambda t: jnp.pad(t[l].reshape(1, SSD_HEADS), ((0, 0), (0, LANES - SSD_HEADS)))
    wo = w_out[l].astype(BF16)
    return {
        "w_in": (wb[:, :q_end], wb[:, q_end:k_end], wb[:, k_end:v_end], wb[:, v_end:z_end], wb[:, z_end:x_end], wdt),
        "g_mix": g_mix[l].reshape(1, d),
        "sinks": attn_sinks[l],
        "g_att": g_attn_out[l].reshape(1, ATT_WIDTH),
        "conv_w": conv_w[l],
        "conv_b": conv_b[l].reshape(1, CONV_DIM),
        "dt_bias": pad_h(dt_bias),
        "a_log": pad_h(a_log),
        "d_skip": d_skip[l],
        "g_ssd": g_ssd[l].reshape(1, SSD_WIDTH),
        "w_out_a": wo[:ATT_WIDTH],
        "w_out_y": wo[ATT_WIDTH:],
        "g_ffn": g_ffn[l].reshape(1, d),
        "wq_t": w_peer_q[l].astype(BF16).T,
        "sub_keys": peer_sub_keys[l].astype(BF16),
        "peer_u": peer_u[l].astype(BF16),
        "peer_vt": peer_v[l].astype(BF16).T,
    }


def kernel(x_prompt, x_sample, c_prompt, c_sample, cache_k, cache_v, state_conv, state_ssm, w_ada, b_ada, g_mix, w_in, attn_sinks, g_attn_out, conv_w, conv_b, dt_bias, a_log, d_skip, g_ssd, w_out, g_ffn, w_peer_q, peer_sub_keys, peer_u, peer_v, g_final):
    depth = w_ada.shape[0]
    assert depth == 1, "one layer per step"
    B, L, D = x_prompt.shape
    Bs, Ls, _ = x_sample.shape
    rows = B + Bs
    rows_pad = -(-rows // 16) * 16
    c_all = jnp.pad(jnp.concatenate([c_prompt, c_sample], axis=0), ((0, rows_pad - rows), (0, 0)))
    gfin = g_final.reshape(1, D)
    hp, hs = x_prompt, x_sample.reshape(1, Bs * Ls, D)
    outs_p, outs_s = [], []
    for l in range(depth):
        lw = _layer_weights(l, w_in, g_mix, attn_sinks, g_attn_out, conv_w, conv_b, dt_bias, a_log, d_skip, g_ssd,
                            w_out, g_ffn, w_peer_q, peer_sub_keys, peer_u, peer_v)
        mod = _mod_call(c_all, w_ada[l], b_ada[l]).reshape(rows_pad, 6, D)
        mods_p = [mod[:B, m].reshape(B, 1, D) for m in range(6)]
        mods_s = [jnp.broadcast_to(mod[B:rows, m][:, None, :], (Bs, Ls, D)).reshape(1, Bs * Ls, D) for m in range(6)]
        final = l == depth - 1
        hp, st_p = _layer(hp, mods_p, None, lw, gfin, B, final)
        hs, st_s = _layer(hs, mods_s, (cache_k[l], cache_v[l], state_conv[l], state_ssm[l]), lw, gfin, Bs, final)
        outs_p.append(st_p)
        outs_s.append(st_s)
    stack = lambda outs, i: jnp.stack([o[i] for o in outs])
    return (hp, hs.reshape(Bs, Ls, D),
            stack(outs_p, 0), stack(outs_p, 1), stack(outs_p, 2), stack(outs_p, 3),
            stack(outs_s, 0), stack(outs_s, 1), stack(outs_s, 2), stack(outs_s, 3))
```

```python
import functools
import math

import jax
import jax.numpy as jnp
from jax import lax
from jax.experimental import pallas as pl
from jax.experimental.pallas import tpu as pltpu

F32 = jnp.float32
BF16 = jnp.bfloat16

EPS = 1e-6
CHUNK = 64
WINDOW = 128
ATT_HEADS = 16
ATT_KV_HEADS = 4
ATT_GROUP = ATT_HEADS // ATT_KV_HEADS
HEAD_DIM = 64
ATT_WIDTH = ATT_HEADS * HEAD_DIM
KV_WIDTH = ATT_KV_HEADS * HEAD_DIM
ATT_SCALE = HEAD_DIM ** -0.5
SSD_HEADS = 16
SSD_HEADDIM = 64
SSD_WIDTH = SSD_HEADS * SSD_HEADDIM
SSD_GROUPS = 4
SSD_HPG = SSD_HEADS // SSD_GROUPS
SSD_STATE = 128
CONV_W = 4
CONV_DIM = SSD_WIDTH + 2 * SSD_GROUPS * SSD_STATE
PEER_HEADS = 8
N_KEYS = 128
D_HALF = 128
PEER_TOPK = 16

LANES = 128
SUBLANES = 8
VMEM_LIMIT = 56 * 1024 * 1024
INV_SQRT2 = 0.7071067811865476
NEG_INF = float("-inf")
POS_INF = float("inf")


def _params(sem):
    return pltpu.CompilerParams(dimension_semantics=sem, vmem_limit_bytes=VMEM_LIMIT)


def _resident(shape):
    nd = len(shape)
    return pl.BlockSpec(shape, lambda *_: (0,) * nd, pipeline_mode=pl.Buffered(1))


def _rms(x, g):
    return x * lax.rsqrt(jnp.mean(x * x, axis=-1, keepdims=True) + EPS) * g


def _silu(x):
    return x * jax.nn.sigmoid(x)


def _softplus(x):
    return jnp.maximum(x, 0.0) + jnp.log1p(jnp.exp(-jnp.abs(x)))


def _dot(a, b):
    return jnp.dot(a, b, preferred_element_type=F32)


def _dot_nt(a, b):
    return lax.dot_general(a, b, (((1,), (1,)), ((), ())), preferred_element_type=F32)


def _mod_kernel(c_ref, w_ref, b_ref, o_ref):
    s = _silu(c_ref[...]).astype(BF16)
    o_ref[...] = _dot(s, w_ref[...].astype(BF16)) + b_ref[...]


def _mod_call(c_all, w_ada, b_ada):
    rows, d = c_all.shape
    n = w_ada.shape[1]
    bn = 1536
    assert n % bn == 0
    return pl.pallas_call(
        _mod_kernel,
        grid=(n // bn,),
        in_specs=[pl.BlockSpec((rows, d), lambda j: (0, 0)),
                  pl.BlockSpec((d, bn), lambda j: (0, j)),
                  pl.BlockSpec((1, bn), lambda j: (0, j))],
        out_specs=pl.BlockSpec((rows, bn), lambda j: (0, j)),
        out_shape=jax.ShapeDtypeStruct((rows, n), F32),
        compiler_params=_params(("arbitrary",)),
        name="mod",
    )(c_all, w_ada, b_ada.reshape(1, n))


def _inproj_kernel(x_ref, sh_ref, sc_ref, g_ref, wq, wk, wv, wz, wx, wdt,
                   q_o, k_o, v_o, z_o, xbc_o, dt_o):
    h = _rms(x_ref[...], g_ref[...]) * (1.0 + sc_ref[...]) + sh_ref[...]
    hb = h.astype(BF16)
    q_o[...] = _dot(hb, wq[...]).astype(q_o.dtype)
    k_o[...] = _dot(hb, wk[...])
    v_o[...] = _dot(hb, wv[...])
    z_o[...] = _dot(hb, wz[...])
    xbc_o[...] = _dot(hb, wx[...])
    dt_o[...] = _dot(hb, wdt[...])


def _mod_spec(arr, tm):
    r = arr.shape[1]
    d = arr.shape[2]
    if r == 1:
        return pl.BlockSpec((None, 1, d), lambda b, i: (b, 0, 0))
    return pl.BlockSpec((None, tm, d), lambda b, i: (b, i, 0))


def _inproj_call(x, sh, sc, g_mix, ws, tm):
    G, L, D = x.shape
    wq, wk, wv, wz, wx, wdt = ws
    tok = lambda n: pl.BlockSpec((None, tm, n), lambda b, i: (b, i, 0))
    out_w = (ATT_WIDTH, KV_WIDTH, KV_WIDTH, SSD_WIDTH, CONV_DIM, LANES)
    out_dt = (BF16, F32, F32, F32, F32, F32)
    return pl.pallas_call(
        _inproj_kernel,
        grid=(G, L // tm),
        in_specs=[tok(D), _mod_spec(sh, tm), _mod_spec(sc, tm), _resident((1, D))]
                 + [_resident(w.shape) for w in ws],
        out_specs=[tok(n) for n in out_w],
        out_shape=[jax.ShapeDtypeStruct((G, L, n), dt) for n, dt in zip(out_w, out_dt)],
        compiler_params=_params(("parallel", "arbitrary")),
        name="inproj",
    )(x, sh, sc, g_mix, *ws)


def _alibi_slope(h):
    return 2.0 ** (-8.0 * (h + 1) / ATT_HEADS)


def _attn_kernel(sink_ref, q_ref, kp_ref, kc_ref, vp_ref, vc_ref, g_ref, o_ref, *, banded, qb):
    blk = pl.program_id(1)
    nk = WINDOW + qb
    r = lax.broadcasted_iota(jnp.int32, (qb, nk), 0)
    j = lax.broadcasted_iota(jnp.int32, (qb, nk), 1)
    dist = jnp.abs(r + WINDOW - j).astype(F32)
    if banded:
        qc = jnp.right_shift(r, 6)
        kc = jnp.right_shift(j, 6) - WINDOW // CHUNK
        valid = (kc <= qc) & (kc >= qc - WINDOW // CHUNK) & ((j >= WINDOW) | (blk > 0))
    q = q_ref[...]
    outs = []
    for kv in range(ATT_KV_HEADS):
        sl = slice(kv * HEAD_DIM, (kv + 1) * HEAD_DIM)
        kb = jnp.concatenate([kp_ref[:, sl], kc_ref[:, sl]], axis=0).astype(BF16)
        vb = jnp.concatenate([vp_ref[:, sl], vc_ref[:, sl]], axis=0).astype(BF16)
        for g in range(ATT_GROUP):
            h = kv * ATT_GROUP + g
            qh = q[:, h * HEAD_DIM:(h + 1) * HEAD_DIM]
            logits = _dot_nt(qh, kb) * ATT_SCALE - _alibi_slope(h) * dist
            if banded:
                logits = jnp.where(valid, logits, NEG_INF)
            sink = sink_ref[h]
            m = jnp.maximum(jnp.max(logits, axis=-1, keepdims=True), sink)
            p = jnp.exp(logits - m)
            den = jnp.sum(p, axis=-1, keepdims=True) + jnp.exp(sink - m)
            outs.append(_dot(p.astype(BF16), vb) / den)
    att = jnp.concatenate(outs, axis=-1)
    o_ref[...] = _rms(att, g_ref[...]).astype(o_ref.dtype)


def _attn_call(q, k, v, k_prev, v_prev, sinks, g_att, qb, banded):
    B, L, _ = q.shape
    nb = L // qb
    wpb = qb // WINDOW
    cur = lambda n: pl.BlockSpec((None, qb, n), lambda b, i: (b, i, 0))
    if k_prev is None:
        prev = pl.BlockSpec((None, WINDOW, KV_WIDTH), lambda b, i: (b, jnp.maximum(i * wpb - 1, 0), 0))
        k_prev, v_prev = k, v
    else:
        prev = pl.BlockSpec((None, WINDOW, KV_WIDTH), lambda b, i: (b, 0, 0))
    return pl.pallas_call(
        functools.partial(_attn_kernel, banded=banded, qb=qb),
        grid=(B, nb),
        in_specs=[pl.BlockSpec(memory_space=pltpu.SMEM),
                  cur(ATT_WIDTH), prev, cur(KV_WIDTH), prev, cur(KV_WIDTH), _resident((1, ATT_WIDTH))],
        out_specs=cur(ATT_WIDTH),
        out_shape=jax.ShapeDtypeStruct((B, L, ATT_WIDTH), BF16),
        compiler_params=_params(("parallel", "arbitrary")),
        name="attn",
    )(sinks, q, k_prev, k, v_prev, v, g_att)


def _ssd_kernel(dskip_ref, xbc_ref, z_ref, dt_ref, tail0_ref, h0_ref, cw_ref, cb_ref, dtb_ref, alog_ref, g_ref,
                y_ref, tail_ref, hT_ref, xext, hT, *, lc):
    i = pl.program_id(1)

    @pl.when(i == 0)
    def _():
        xext[0:SUBLANES, :] = tail0_ref[...]
        hT[...] = h0_ref[...]

    xext[SUBLANES:SUBLANES + lc, :] = xbc_ref[...]
    acc = cb_ref[...]
    for t in range(CONV_W):
        off = SUBLANES - (CONV_W - 1) + t
        acc = acc + xext[off:off + lc, :] * cw_ref[t:t + 1, :]
    xa = _silu(acc)
    tail = xext[lc:lc + SUBLANES, :]
    xext[0:SUBLANES, :] = tail
    tail_ref[...] = tail

    xs = xa[:, :SSD_WIDTH]
    bm = xa[:, SSD_WIDTH:SSD_WIDTH + SSD_GROUPS * SSD_STATE]
    cm = xa[:, SSD_WIDTH + SSD_GROUPS * SSD_STATE:]
    dt = _softplus(dt_ref[...] + dtb_ref[...])
    da = dt * (-jnp.exp(alog_ref[...]))
    row = lax.broadcasted_iota(jnp.int32, (lc, lc), 0)
    col = lax.broadcasted_iota(jnp.int32, (lc, lc), 1)
    causal = row >= col
    acum = jnp.dot(causal.astype(F32), da, precision=lax.Precision.HIGHEST, preferred_element_type=F32)
    acum_t = acum.T
    a_last = acum[lc - 1:lc, :]
    dec_end = jnp.exp(a_last - acum)
    eac = jnp.exp(acum)
    cdec = jnp.exp(a_last)
    ys = []
    for g in range(SSD_GROUPS):
        bg = bm[:, g * SSD_STATE:(g + 1) * SSD_STATE]
        cgb = cm[:, g * SSD_STATE:(g + 1) * SSD_STATE].astype(BF16)
        cb = _dot_nt(cgb, bg.astype(BF16))
        bgt = bg.T.astype(BF16)
        for r in range(SSD_HPG):
            h = g * SSD_HPG + r
            seg = acum[:, h:h + 1] - acum_t[h:h + 1, :]
            decay = jnp.exp(jnp.where(causal, seg, NEG_INF))
            xh = xs[:, h * SSD_HEADDIM:(h + 1) * SSD_HEADDIM]
            xdt = xh * dt[:, h:h + 1]
            hprev = hT[h]
            y = _dot((cb * decay).astype(BF16), xdt.astype(BF16))
            y = y + _dot(cgb, hprev.astype(BF16)) * eac[:, h:h + 1]
            st = _dot(bgt, (xdt * dec_end[:, h:h + 1]).astype(BF16))
            hT[h] = hprev * cdec[:, h:h + 1] + st
            ys.append(y + dskip_ref[h] * xh)
    y = jnp.concatenate(ys, axis=-1) * _silu(z_ref[...])
    y_ref[...] = _rms(y, g_ref[...]).astype(y_ref.dtype)
    hT_ref[...] = hT[...]


def _ssd_call(xbc, z, dt, tail0, h0t, conv_w, conv_b, dt_bias, a_log, d_skip, g_ssd, lc):
    B, L, _ = xbc.shape
    tok = lambda n: pl.BlockSpec((None, lc, n), lambda b, i: (b, i, 0))
    per_b = lambda shp: pl.BlockSpec((None,) + shp, lambda b, i: (b,) + (0,) * len(shp))
    return pl.pallas_call(
        functools.partial(_ssd_kernel, lc=lc),
        grid=(B, L // lc),
        in_specs=[pl.BlockSpec(memory_space=pltpu.SMEM),
                  tok(CONV_DIM), tok(SSD_WIDTH), tok(LANES),
                  per_b((SUBLANES, CONV_DIM)), per_b((SSD_HEADS, SSD_STATE, SSD_HEADDIM)),
                  _resident((CONV_W, CONV_DIM)), _resident((1, CONV_DIM)),
                  _resident((1, LANES)), _resident((1, LANES)), _resident((1, SSD_WIDTH))],
        out_specs=[tok(SSD_WIDTH), per_b((SUBLANES, CONV_DIM)), per_b((SSD_HEADS, SSD_STATE, SSD_HEADDIM))],
        out_shape=[jax.ShapeDtypeStruct((B, L, SSD_WIDTH), BF16),
                   jax.ShapeDtypeStruct((B, SUBLANES, CONV_DIM), F32),
                   jax.ShapeDtypeStruct((B, SSD_HEADS, SSD_STATE, SSD_HEADDIM), F32)],
        scratch_shapes=[pltpu.VMEM((lc + SUBLANES, CONV_DIM), F32),
                        pltpu.VMEM((SSD_HEADS, SSD_STATE, SSD_HEADDIM), F32)],
        compiler_params=_params(("parallel", "arbitrary")),
        name="ssd",
    )(d_skip, xbc, z, dt, tail0, h0t, conv_w, conv_b, dt_bias, a_log, g_ssd)


def _outproj_kernel(att_ref, y_ref, x_ref, gt_ref, sh_ref, sc_ref, g_ref, wa, wy, x1_o, h2_o):
    mixed = _dot(att_ref[...], wa[...]) + _dot(y_ref[...], wy[...])
    x1 = x_ref[...] + gt_ref[...] * mixed
    x1_o[...] = x1
    h2_o[...] = (_rms(x1, g_ref[...]) * (1.0 + sc_ref[...]) + sh_ref[...]).astype(h2_o.dtype)


def _outproj_call(att, y, x, gt, sh, sc, g_ffn, wa, wy, tm):
    G, L, D = x.shape
    tok = lambda n: pl.BlockSpec((None, tm, n), lambda b, i: (b, i, 0))
    return pl.pallas_call(
        _outproj_kernel,
        grid=(G, L // tm),
        in_specs=[tok(ATT_WIDTH), tok(SSD_WIDTH), tok(D), _mod_spec(gt, tm), _mod_spec(sh, tm), _mod_spec(sc, tm),
                  _resident((1, D)), _resident(wa.shape), _resident(wy.shape)],
        out_specs=[tok(D), tok(D)],
        out_shape=[jax.ShapeDtypeStruct((G, L, D), F32), jax.ShapeDtypeStruct((G, L, D), BF16)],
        compiler_params=_params(("parallel", "arbitrary")),
        name="outproj",
    )(att, y, x, gt, sh, sc, g_ffn, wa, wy)


def _top_rows(s, k):
    rows = []
    cur = s
    for _ in range(k):
        m = jnp.max(cur, axis=0, keepdims=True)
        rows.append(m)
        cur = jnp.where(cur == m, NEG_INF, cur)
    return rows


def _pscore_kernel(h2_ref, wqt_ref, sk_ref, a_o, th_o, b_o, s2_o):
    qt = _dot_nt(wqt_ref[...], h2_ref[...])
    ts = qt.shape[1]
    sub = lax.broadcasted_iota(jnp.int32, (PEER_TOPK, ts), 0)
    for h in range(PEER_HEADS):
        base = h * 2 * D_HALF
        s1 = _dot(sk_ref[h, 0], qt[base:base + D_HALF].astype(BF16))
        s2 = _dot(sk_ref[h, 1], qt[base + D_HALF:base + 2 * D_HALF].astype(BF16))
        v1 = _top_rows(s1, PEER_TOPK)
        v2 = _top_rows(s2, PEER_TOPK)
        v1m = jnp.concatenate(v1, axis=0)
        cands = [jnp.where(sub < PEER_TOPK // (jj + 1), v1m + v2[jj], NEG_INF) for jj in range(PEER_TOPK)]
        cur = cands
        tau = None
        for _ in range(PEER_TOPK):
            m = cur[0]
            for c in cur[1:]:
                m = jnp.maximum(m, c)
            tau = jnp.max(m, axis=0, keepdims=True)
            cur = [jnp.where(c == tau, NEG_INF, c) for c in cur]
        a1 = jnp.exp(v1m - v1[0])
        zsum = jnp.zeros_like(a1)
        for jj in range(PEER_TOPK):
            zsum = zsum + jnp.where(cands[jj] >= tau, a1 * jnp.exp(v2[jj] - v2[0]), 0.0)
        z = jnp.sum(zsum, axis=0, keepdims=True)
        th = jnp.full(s1.shape, POS_INF, F32)
        for jj in range(PEER_TOPK):
            th = jnp.minimum(th, jnp.where(s1 + v2[jj] >= tau, v2[jj], POS_INF))
        a_o[h] = jnp.exp(s1 - v1[0]) / z
        th_o[h] = th
        b_o[h] = jnp.exp(s2 - v2[0])
        s2_o[h] = s2


def _pscore_call(h2, wqt, sk, ts):
    T, D = h2.shape
    out = jax.ShapeDtypeStruct((PEER_HEADS, N_KEYS, T), F32)
    ospec = pl.BlockSpec((PEER_HEADS, N_KEYS, ts), lambda i: (0, 0, i))
    return pl.pallas_call(
        _pscore_kernel,
        grid=(T // ts,),
        in_specs=[pl.BlockSpec((ts, D), lambda i: (i, 0)), _resident(wqt.shape), _resident(sk.shape)],
        out_specs=[ospec] * 4,
        out_shape=[out] * 4,
        compiler_params=_params(("parallel",)),
        name="pscore",
    )(h2, wqt, sk)


GATE_ROWS = 32
MXU_TILE = 256


def _dependent_zero(x):
    bits = pltpu.bitcast(x[0:SUBLANES, :], jnp.uint32)
    zero = lax.shift_right_logical(lax.shift_right_logical(bits, jnp.uint32(16)), jnp.uint32(16))
    return pltpu.bitcast(zero, BF16)


def _gate_tile(s2_ref, b_ref, a_ref, th_ref, n1b, g_scr, c, r):
    lanes = slice(c * LANES, (c + 1) * LANES)
    rows = slice(r * GATE_ROWS, (r + 1) * GATE_ROWS)
    accs = [None] * n1b
    for h in range(PEER_HEADS):
        s2t = s2_ref[h, rows, lanes]
        bt = b_ref[h, rows, lanes]
        for k in range(n1b):
            c_hk = jnp.where(s2t >= th_ref[h, k:k + 1, lanes], a_ref[h, k:k + 1, lanes] * bt, 0.0)
            accs[k] = c_hk if accs[k] is None else accs[k] + c_hk
    for k in range(n1b):
        g_scr[k * N_KEYS + r * GATE_ROWS:k * N_KEYS + (r + 1) * GATE_ROWS, lanes] = accs[k]
    return accs[n1b - 1]


def _pdense_kernel(h2_ref, u_ref, vt_ref, a_ref, th_ref, b_ref, s2_ref, x1_ref, gt_ref, gf_ref,
                   y_ref, h2t, acc, act_scr, g_scr, w_scr, *, n1b):
    i = pl.program_id(0)
    e = pl.program_id(1)
    d, tt = acc.shape

    @pl.when(e == 0)
    def _():
        h2t[...] = h2_ref[...].astype(F32).T.astype(BF16)
        acc[...] = jnp.zeros_like(acc)

    @pl.when(e >= 0)
    def _():
        tiles = [(c, r) for c in range(tt // LANES) for r in range(N_KEYS // GATE_ROWS)]
        nkt = d // MXU_TILE
        assert len(tiles) % nkt == 0
        per = len(tiles) // nkt
        act = None
        for k in range(nkt):
            for (c, r) in tiles[k * per:(k + 1) * per]:
                last = _gate_tile(s2_ref, b_ref, a_ref, th_ref, n1b, g_scr, c, r)
            wk = h2t[k * MXU_TILE:(k + 1) * MXU_TILE, :]
            top = wk[0:16, :]
            top = jnp.concatenate([top[:, 0:LANES] + _dependent_zero(last), top[:, LANES:]], axis=1)
            wk = jnp.concatenate([top, wk[16:]], axis=0)
            part = _dot(u_ref[:, k * MXU_TILE:(k + 1) * MXU_TILE], wk)
            act = part if act is None else act + part
        act_scr[...] = act

    @pl.when(i >= 0)
    def _():
        act = act_scr[...]
        gel = 0.5 * act * (1.0 + lax.erf(act * INV_SQRT2))
        w_scr[...] = (gel * g_scr[...]).astype(BF16)
        acc[...] += _dot(vt_ref[...], w_scr[...])

    @pl.when(e == pl.num_programs(1) - 1)
    def _():
        y = x1_ref[...] + gt_ref[...] * acc[...].T
        y_ref[...] = _rms(y, gf_ref[...])


def _pdense_call(h2, u, vt, a, th, b, s2, x1, gt, g_final, tt, n1b):
    T, D = h2.shape
    eb = n1b * N_KEYS
    nblk = u.shape[0] // eb
    tpg = T // tt // gt.shape[0]
    if gt.shape[1] == 1:
        gt_spec = pl.BlockSpec((None, 1, D), lambda i, e: (i // tpg, 0, 0))
    else:
        gt_spec = pl.BlockSpec((None, tt, D), lambda i, e: (i // tpg, i % tpg, 0))
    a = a.reshape(PEER_HEADS, N_KEYS // n1b, n1b, T)
    th = th.reshape(PEER_HEADS, N_KEYS // n1b, n1b, T)
    sel = pl.BlockSpec((PEER_HEADS, None, n1b, tt), lambda i, e: (0, e, 0, i))
    full = pl.BlockSpec((PEER_HEADS, N_KEYS, tt), lambda i, e: (0, 0, i))
    tok = pl.BlockSpec((tt, D), lambda i, e: (i, 0))
    return pl.pallas_call(
        functools.partial(_pdense_kernel, n1b=n1b),
        grid=(T // tt, nblk),
        in_specs=[tok, pl.BlockSpec((eb, D), lambda i, e: (e, 0)), pl.BlockSpec((D, eb), lambda i, e: (0, e)),
                  sel, sel, full, full, tok, gt_spec, _resident((1, D))],
        out_specs=tok,
        out_shape=jax.ShapeDtypeStruct((T, D), F32),
        scratch_shapes=[pltpu.VMEM((D, tt), BF16), pltpu.VMEM((D, tt), F32), pltpu.VMEM((eb, tt), F32),
                        pltpu.VMEM((eb, tt), F32), pltpu.VMEM((eb, tt), BF16)],
        compiler_params=_params(("parallel", "arbitrary")),
        name="pdense",
    )(h2, u, vt, a, th, b, s2, x1, gt, g_final)


def _tile(n, pref):
    t = min(n, pref)
    assert n % t == 0, (n, pref)
    return t


def _layer(x, mods, past, lw, g_final, batch, final):
    G, L, D = x.shape
    T = G * L
    sh1, sc1, gt1, sh2, sc2, gt2 = mods
    tm = _tile(L, 256)
    q, k, v, z, xbc, dt = _inproj_call(x, sh1, sc1, lw["g_mix"], lw["w_in"], tm)
    seq = T // batch
    bs = lambda t: t.reshape(batch, seq, t.shape[-1])
    q, k, v, z, xbc, dt = [bs(t) for t in (q, k, v, z, xbc, dt)]
    if past is None:
        att = _attn_call(q, k, v, None, None, lw["sinks"], lw["g_att"], _tile(seq, 256), True)
        tail0 = jnp.zeros((batch, SUBLANES, CONV_DIM), F32)
        h0t = jnp.zeros((batch, SSD_HEADS, SSD_STATE, SSD_HEADDIM), F32)
        lc = _tile(seq, 128)
        new_k, new_v = k[:, -WINDOW:], v[:, -WINDOW:]
    else:
        cache_k, cache_v, conv_prev, h0 = past
        att = _attn_call(q, k, v, cache_k.reshape(batch, WINDOW, KV_WIDTH), cache_v.reshape(batch, WINDOW, KV_WIDTH),
                         lw["sinks"], lw["g_att"], seq, False)
        tail0 = jnp.pad(conv_prev, ((0, 0), (SUBLANES - (CONV_W - 1), 0), (0, 0)))
        h0t = jnp.swapaxes(h0, -1, -2)
        lc = seq
        new_k, new_v = k, v
    y_ssd, tail, ht = _ssd_call(xbc, z, dt, tail0, h0t, lw["conv_w"], lw["conv_b"], lw["dt_bias"], lw["a_log"],
                                lw["d_skip"], lw["g_ssd"], lc)
    gs = lambda t: t.reshape(G, L, t.shape[-1])
    x1, h2 = _outproj_call(gs(att), gs(y_ssd), x, gt1, sh2, sc2, lw["g_ffn"], lw["w_out_a"], lw["w_out_y"], tm)
    h2f = h2.reshape(T, D)
    ts = _tile(T, 256)
    a, th, b, s2 = _pscore_call(h2f, lw["wq_t"], lw["sub_keys"], ts)
    tt = _tile(T, 512)
    gf = g_final if final else jnp.ones_like(g_final)
    y = _pdense_call(h2f, lw["peer_u"], lw["peer_vt"], a, th, b, s2, x1.reshape(T, D), gt2, gf, tt, 4)
    assert final
    new_k = new_k.reshape(batch, -1, ATT_KV_HEADS, HEAD_DIM)
    new_v = new_v.reshape(batch, -1, ATT_KV_HEADS, HEAD_DIM)
    conv_state = tail[:, SUBLANES - (CONV_W - 1):, :]
    ssm = jnp.swapaxes(ht, -1, -2)
    return y.reshape(G, L, D), (new_k, new_v, conv_state, ssm)


def _layer_weights(l, w_in, g_mix, attn_sinks, g_attn_out, conv_w, conv_b, dt_bias, a_log, d_skip, g_ssd, w_out,
                   g_ffn, w_peer_q, peer_sub_keys, peer_u, peer_v):
    d = w_in.shape[1]
    wb = w_in[l].astype(BF16)
    q_end = ATT_WIDTH
    k_end = q_end + KV_WIDTH
    v_end = k_end + KV_WIDTH
    z_end = v_end + SSD_WIDTH
    x_end = z_end + CONV_DIM
    wdt = jnp.pad(wb[:, x_end:], ((0, 0), (0, LANES - SSD_HEADS)))
    pad_h = lambda t: jnp.pad(t[l].reshape(1, SSD_HEADS), ((0, 0), (0, LANES - SSD_HEADS)))
    wo = w_out[l].astype(BF16)
    return {
        "w_in": (wb[:, :q_end], wb[:, q_end:k_end], wb[:, k_end:v_end], wb[:, v_end:z_end], wb[:, z_end:x_end], wdt),
        "g_mix": g_mix[l].reshape(1, d),
        "sinks": attn_sinks[l],
        "g_att": g_attn_out[l].reshape(1, ATT_WIDTH),
        "conv_w": conv_w[l],
        "conv_b": conv_b[l].reshape(1, CONV_DIM),
        "dt_bias": pad_h(dt_bias),
        "a_log": pad_h(a_log),
        "d_skip": d_skip[l],
        "g_ssd": g_ssd[l].reshape(1, SSD_WIDTH),
        "w_out_a": wo[:ATT_WIDTH],
        "w_out_y": wo[ATT_WIDTH:],
        "g_ffn": g_ffn[l].reshape(1, d),
        "wq_t": w_peer_q[l].astype(BF16).T,
        "sub_keys": peer_sub_keys[l].astype(BF16),
        "peer_u": peer_u[l].astype(BF16),
        "peer_vt": peer_v[l].astype(BF16).T,
    }


def kernel(x_prompt, x_sample, c_prompt, c_sample, cache_k, cache_v, state_conv, state_ssm, w_ada, b_ada, g_mix, w_in, attn_sinks, g_attn_out, conv_w, conv_b, dt_bias, a_log, d_skip, g_ssd, w_out, g_ffn, w_peer_q, peer_sub_keys, peer_u, peer_v, g_final):
    depth = w_ada.shape[0]
    assert depth == 1, "one layer per step"
    B, L, D = x_prompt.shape
    Bs, Ls, _ = x_sample.shape
    rows = B + Bs
    rows_pad = -(-rows // 16) * 16
    c_all = jnp.pad(jnp.concatenate([c_prompt, c_sample], axis=0), ((0, rows_pad - rows), (0, 0)))
    gfin = g_final.reshape(1, D)
    hp, hs = x_prompt, x_sample.reshape(1, Bs * Ls, D)
    outs_p, outs_s = [], []
    for l in range(depth):
        lw = _layer_weights(l, w_in, g_mix, attn_sinks, g_attn_out, conv_w, conv_b, dt_bias, a_log, d_skip, g_ssd,
                            w_out, g_ffn, w_peer_q, peer_sub_keys, peer_u, peer_v)
        mod = _mod_call(c_all, w_ada[l], b_ada[l]).reshape(rows_pad, 6, D)
        mods_p = [mod[:B, m].reshape(B, 1, D) for m in range(6)]
        mods_s = [jnp.broadcast_to(mod[B:rows, m][:, None, :], (Bs, Ls, D)).reshape(1, Bs * Ls, D) for m in range(6)]
        final = l == depth - 1
        hp, st_p = _layer(hp, mods_p, None, lw, gfin, B, final)
        hs, st_s = _layer(hs, mods_s, (cache_k[l], cache_v[l], state_conv[l], state_ssm[l]), lw, gfin, Bs, final)
        outs_p.append(st_p)
        outs_s.append(st_s)
    stack = lambda outs, i: jnp.stack([o[i] for o in outs])
    return (hp, hs.reshape(Bs, Ls, D),
            stack(outs_p, 0), stack(outs_p, 1), stack(outs_p, 2), stack(outs_p, 3),
            stack(outs_s, 0), stack(outs_s, 1), stack(outs_s, 2), stack(outs_s, 3))
```

```python
import functools
import math

import jax
import jax.numpy as jnp
from jax import lax
from jax.experimental import pallas as pl
from jax.experimental.pallas import tpu as pltpu

F32 = jnp.float32
BF16 = jnp.bfloat16

EPS = 1e-6
CHUNK = 64
WINDOW = 128
ATT_HEADS = 16
ATT_KV_HEADS = 4
ATT_GROUP = ATT_HEADS // ATT_KV_HEADS
HEAD_DIM = 64
ATT_WIDTH = ATT_HEADS * HEAD_DIM
KV_WIDTH = ATT_KV_HEADS * HEAD_DIM
ATT_SCALE = HEAD_DIM ** -0.5
SSD_HEADS = 16
SSD_HEADDIM = 64
SSD_WIDTH = SSD_HEADS * SSD_HEADDIM
SSD_GROUPS = 4
SSD_HPG = SSD_HEADS // SSD_GROUPS
SSD_STATE = 128
CONV_W = 4
CONV_DIM = SSD_WIDTH + 2 * SSD_GROUPS * SSD_STATE
PEER_HEADS = 8
N_KEYS = 128
D_HALF = 128
PEER_TOPK = 16

LANES = 128
SUBLANES = 8
VMEM_LIMIT = 56 * 1024 * 1024
INV_SQRT2 = 0.7071067811865476
NEG_INF = float("-inf")
POS_INF = float("inf")


def _params(sem):
    return pltpu.CompilerParams(dimension_semantics=sem, vmem_limit_bytes=VMEM_LIMIT)


def _resident(shape):
    nd = len(shape)
    return pl.BlockSpec(shape, lambda *_: (0,) * nd, pipeline_mode=pl.Buffered(1))


def _rms(x, g):
    return x * lax.rsqrt(jnp.mean(x * x, axis=-1, keepdims=True) + EPS) * g


def _silu(x):
    return x * jax.nn.sigmoid(x)


def _softplus(x):
    return jnp.maximum(x, 0.0) + jnp.log1p(jnp.exp(-jnp.abs(x)))


def _dot(a, b):
    return jnp.dot(a, b, preferred_element_type=F32)


def _dot_nt(a, b):
    return lax.dot_general(a, b, (((1,), (1,)), ((), ())), preferred_element_type=F32)


def _mod_kernel(c_ref, w_ref, b_ref, o_ref):
    s = _silu(c_ref[...]).astype(BF16)
    o_ref[...] = _dot(s, w_ref[...].astype(BF16)) + b_ref[...]


def _mod_call(c_all, w_ada, b_ada):
    rows, d = c_all.shape
    n = w_ada.shape[1]
    bn = 1536
    assert n % bn == 0
    return pl.pallas_call(
        _mod_kernel,
        grid=(n // bn,),
        in_specs=[pl.BlockSpec((rows, d), lambda j: (0, 0)),
                  pl.BlockSpec((d, bn), lambda j: (0, j)),
                  pl.BlockSpec((1, bn), lambda j: (0, j))],
        out_specs=pl.BlockSpec((rows, bn), lambda j: (0, j)),
        out_shape=jax.ShapeDtypeStruct((rows, n), F32),
        compiler_params=_params(("arbitrary",)),
        name="mod",
    )(c_all, w_ada, b_ada.reshape(1, n))


def _inproj_kernel(x_ref, sh_ref, sc_ref, g_ref, wq, wk, wv, wz, wx, wdt,
                   q_o, k_o, v_o, z_o, xbc_o, dt_o):
    h = _rms(x_ref[...], g_ref[...]) * (1.0 + sc_ref[...]) + sh_ref[...]
    hb = h.astype(BF16)
    q_o[...] = _dot(hb, wq[...]).astype(q_o.dtype)
    k_o[...] = _dot(hb, wk[...])
    v_o[...] = _dot(hb, wv[...])
    z_o[...] = _dot(hb, wz[...])
    xbc_o[...] = _dot(hb, wx[...])
    dt_o[...] = _dot(hb, wdt[...])


def _mod_spec(arr, tm):
    r = arr.shape[1]
    d = arr.shape[2]
    if r == 1:
        return pl.BlockSpec((None, 1, d), lambda b, i: (b, 0, 0))
    return pl.BlockSpec((None, tm, d), lambda b, i: (b, i, 0))


def _inproj_call(x, sh, sc, g_mix, ws, tm):
    G, L, D = x.shape
    wq, wk, wv, wz, wx, wdt = ws
    tok = lambda n: pl.BlockSpec((None, tm, n), lambda b, i: (b, i, 0))
    out_w = (ATT_WIDTH, KV_WIDTH, KV_WIDTH, SSD_WIDTH, CONV_DIM, LANES)
    out_dt = (BF16, F32, F32, F32, F32, F32)
    return pl.pallas_call(
        _inproj_kernel,
        grid=(G, L // tm),
        in_specs=[tok(D), _mod_spec(sh, tm), _mod_spec(sc, tm), _resident((1, D))]
                 + [_resident(w.shape) for w in ws],
        out_specs=[tok(n) for n in out_w],
        out_shape=[jax.ShapeDtypeStruct((G, L, n), dt) for n, dt in zip(out_w, out_dt)],
        compiler_params=_params(("parallel", "arbitrary")),
        name="inproj",
    )(x, sh, sc, g_mix, *ws)


def _alibi_slope(h):
    return 2.0 ** (-8.0 * (h + 1) / ATT_HEADS)


def _attn_kernel(sink_ref, q_ref, kp_ref, kc_ref, vp_ref, vc_ref, g_ref, o_ref, *, banded, qb):
    blk = pl.program_id(1)
    nk = WINDOW + qb
    r = lax.broadcasted_iota(jnp.int32, (qb, nk), 0)
    j = lax.broadcasted_iota(jnp.int32, (qb, nk), 1)
    dist = jnp.abs(r + WINDOW - j).astype(F32)
    if banded:
        qc = jnp.right_shift(r, 6)
        kc = jnp.right_shift(j, 6) - WINDOW // CHUNK
        valid = (kc <= qc) & (kc >= qc - WINDOW // CHUNK) & ((j >= WINDOW) | (blk > 0))
    q = q_ref[...]
    outs = []
    for kv in range(ATT_KV_HEADS):
        sl = slice(kv * HEAD_DIM, (kv + 1) * HEAD_DIM)
        kb = jnp.concatenate([kp_ref[:, sl], kc_ref[:, sl]], axis=0).astype(BF16)
        vb = jnp.concatenate([vp_ref[:, sl], vc_ref[:, sl]], axis=0).astype(BF16)
        for g in range(ATT_GROUP):
            h = kv * ATT_GROUP + g
            qh = q[:, h * HEAD_DIM:(h + 1) * HEAD_DIM]
            logits = _dot_nt(qh, kb) * ATT_SCALE - _alibi_slope(h) * dist
            if banded:
                logits = jnp.where(valid, logits, NEG_INF)
            sink = sink_ref[h]
            m = jnp.maximum(jnp.max(logits, axis=-1, keepdims=True), sink)
            p = jnp.exp(logits - m)
            den = jnp.sum(p, axis=-1, keepdims=True) + jnp.exp(sink - m)
            outs.append(_dot(p.astype(BF16), vb) / den)
    att = jnp.concatenate(outs, axis=-1)
    o_ref[...] = _rms(att, g_ref[...]).astype(o_ref.dtype)


def _attn_call(q, k, v, k_prev, v_prev, sinks, g_att, qb, banded):
    B, L, _ = q.shape
    nb = L // qb
    wpb = qb // WINDOW
    cur = lambda n: pl.BlockSpec((None, qb, n), lambda b, i: (b, i, 0))
    if k_prev is None:
        prev = pl.BlockSpec((None, WINDOW, KV_WIDTH), lambda b, i: (b, jnp.maximum(i * wpb - 1, 0), 0))
        k_prev, v_prev = k, v
    else:
        prev = pl.BlockSpec((None, WINDOW, KV_WIDTH), lambda b, i: (b, 0, 0))
    return pl.pallas_call(
        functools.partial(_attn_kernel, banded=banded, qb=qb),
        grid=(B, nb),
        in_specs=[pl.BlockSpec(memory_space=pltpu.SMEM),
                  cur(ATT_WIDTH), prev, cur(KV_WIDTH), prev, cur(KV_WIDTH), _resident((1, ATT_WIDTH))],
        out_specs=cur(ATT_WIDTH),
        out_shape=jax.ShapeDtypeStruct((B, L, ATT_WIDTH), BF16),
        compiler_params=_params(("parallel", "arbitrary")),
        name="attn",
    )(sinks, q, k_prev, k, v_prev, v, g_att)


def _ssd_kernel(dskip_ref, xbc_ref, z_ref, dt_ref, tail0_ref, h0_ref, cw_ref, cb_ref, dtb_ref, alog_ref, g_ref,
                y_ref, tail_ref, hT_ref, xext, hT, *, lc):
    i = pl.program_id(1)

    @pl.when(i == 0)
    def _():
        xext[0:SUBLANES, :] = tail0_ref[...]
        hT[...] = h0_ref[...]

    xext[SUBLANES:SUBLANES + lc, :] = xbc_ref[...]
    acc = cb_ref[...]
    for t in range(CONV_W):
        off = SUBLANES - (CONV_W - 1) + t
        acc = acc + xext[off:off + lc, :] * cw_ref[t:t + 1, :]
    xa = _silu(acc)
    tail = xext[lc:lc + SUBLANES, :]
    xext[0:SUBLANES, :] = tail
    tail_ref[...] = tail

    xs = xa[:, :SSD_WIDTH]
    bm = xa[:, SSD_WIDTH:SSD_WIDTH + SSD_GROUPS * SSD_STATE]
    cm = xa[:, SSD_WIDTH + SSD_GROUPS * SSD_STATE:]
    dt = _softplus(dt_ref[...] + dtb_ref[...])
    da = dt * (-jnp.exp(alog_ref[...]))
    row = lax.broadcasted_iota(jnp.int32, (lc, lc), 0)
    col = lax.broadcasted_iota(jnp.int32, (lc, lc), 1)
    causal = row >= col
    acum = jnp.dot(causal.astype(F32), da, precision=lax.Precision.HIGHEST, preferred_element_type=F32)
    acum_t = acum.T
    a_last = acum[lc - 1:lc, :]
    dec_end = jnp.exp(a_last - acum)
    eac = jnp.exp(acum)
    cdec = jnp.exp(a_last)
    ys = []
    for g in range(SSD_GROUPS):
        bg = bm[:, g * SSD_STATE:(g + 1) * SSD_STATE]
        cgb = cm[:, g * SSD_STATE:(g + 1) * SSD_STATE].astype(BF16)
        cb = _dot_nt(cgb, bg.astype(BF16))
        bgt = bg.T.astype(BF16)
        for r in range(SSD_HPG):
            h = g * SSD_HPG + r
            seg = acum[:, h:h + 1] - acum_t[h:h + 1, :]
            decay = jnp.exp(jnp.where(causal, seg, NEG_INF))
            xh = xs[:, h * SSD_HEADDIM:(h + 1) * SSD_HEADDIM]
            xdt = xh * dt[:, h:h + 1]
            hprev = hT[h]
            y = _dot((cb * decay).astype(BF16), xdt.astype(BF16))
            y = y + _dot(cgb, hprev.astype(BF16)) * eac[:, h:h + 1]
            st = _dot(bgt, (xdt * dec_end[:, h:h + 1]).astype(BF16))
            hT[h] = hprev * cdec[:, h:h + 1] + st
            ys.append(y + dskip_ref[h] * xh)
    y = jnp.concatenate(ys, axis=-1) * _silu(z_ref[...])
    y_ref[...] = _rms(y, g_ref[...]).astype(y_ref.dtype)
    hT_ref[...] = hT[...]


def _ssd_call(xbc, z, dt, tail0, h0t, conv_w, conv_b, dt_bias, a_log, d_skip, g_ssd, lc):
    B, L, _ = xbc.shape
    tok = lambda n: pl.BlockSpec((None, lc, n), lambda b, i: (b, i, 0))
    per_b = lambda shp: pl.BlockSpec((None,) + shp, lambda b, i: (b,) + (0,) * len(shp))
    return pl.pallas_call(
        functools.partial(_ssd_kernel, lc=lc),
        grid=(B, L // lc),
        in_specs=[pl.BlockSpec(memory_space=pltpu.SMEM),
                  tok(CONV_DIM), tok(SSD_WIDTH), tok(LANES),
                  per_b((SUBLANES, CONV_DIM)), per_b((SSD_HEADS, SSD_STATE, SSD_HEADDIM)),
                  _resident((CONV_W, CONV_DIM)), _resident((1, CONV_DIM)),
                  _resident((1, LANES)), _resident((1, LANES)), _resident((1, SSD_WIDTH))],
        out_specs=[tok(SSD_WIDTH), per_b((SUBLANES, CONV_DIM)), per_b((SSD_HEADS, SSD_STATE, SSD_HEADDIM))],
        out_shape=[jax.ShapeDtypeStruct((B, L, SSD_WIDTH), BF16),
                   jax.ShapeDtypeStruct((B, SUBLANES, CONV_DIM), F32),
                   jax.ShapeDtypeStruct((B, SSD_HEADS, SSD_STATE, SSD_HEADDIM), F32)],
        scratch_shapes=[pltpu.VMEM((lc + SUBLANES, CONV_DIM), F32),
                        pltpu.VMEM((SSD_HEADS, SSD_STATE, SSD_HEADDIM), F32)],
        compiler_params=_params(("parallel", "arbitrary")),
        name="ssd",
    )(d_skip, xbc, z, dt, tail0, h0t, conv_w, conv_b, dt_bias, a_log, g_ssd)


def _outproj_kernel(att_ref, y_ref, x_ref, gt_ref, sh_ref, sc_ref, g_ref, wa, wy, x1_o, h2_o):
    mixed = _dot(att_ref[...], wa[...]) + _dot(y_ref[...], wy[...])
    x1 = x_ref[...] + gt_ref[...] * mixed
    x1_o[...] = x1
    h2_o[...] = (_rms(x1, g_ref[...]) * (1.0 + sc_ref[...]) + sh_ref[...]).astype(h2_o.dtype)


def _outproj_call(att, y, x, gt, sh, sc, g_ffn, wa, wy, tm):
    G, L, D = x.shape
    tok = lambda n: pl.BlockSpec((None, tm, n), lambda b, i: (b, i, 0))
    return pl.pallas_call(
        _outproj_kernel,
        grid=(G, L // tm),
        in_specs=[tok(ATT_WIDTH), tok(SSD_WIDTH), tok(D), _mod_spec(gt, tm), _mod_spec(sh, tm), _mod_spec(sc, tm),
                  _resident((1, D)), _resident(wa.shape), _resident(wy.shape)],
        out_specs=[tok(D), tok(D)],
        out_shape=[jax.ShapeDtypeStruct((G, L, D), F32), jax.ShapeDtypeStruct((G, L, D), BF16)],
        compiler_params=_params(("parallel", "arbitrary")),
        name="outproj",
    )(att, y, x, gt, sh, sc, g_ffn, wa, wy)


def _top_rows(s, k):
    rows = []
    cur = s
    for _ in range(k):
        m = jnp.max(cur, axis=0, keepdims=True)
        rows.append(m)
        cur = jnp.where(cur == m, NEG_INF, cur)
    return rows


def _pscore_kernel(h2_ref, wqt_ref, sk_ref, a_o, th_o, b_o, s2_o):
    qt = _dot_nt(wqt_ref[...], h2_ref[...])
    ts = qt.shape[1]
    sub8 = lax.broadcasted_iota(jnp.int32, (SUBLANES, ts), 0)
    for h in range(PEER_HEADS):
        base = h * 2 * D_HALF
        s1 = _dot(sk_ref[h, 0], qt[base:base + D_HALF].astype(BF16))
        s2 = _dot(sk_ref[h, 1], qt[base + D_HALF:base + 2 * D_HALF].astype(BF16))
        v1 = _top_rows(s1, PEER_TOPK)
        v2 = _top_rows(s2, PEER_TOPK)
        v1m = jnp.concatenate(v1, axis=0)
        v1h = jnp.concatenate(v1[0:SUBLANES], axis=0)
        cands = [v1m + v2[0]] + [jnp.where(sub8 < PEER_TOPK // (jj + 1), v1h + v2[jj], NEG_INF)
                                 for jj in range(1, PEER_TOPK)]
        cur = cands
        tau = None
        for _ in range(PEER_TOPK):
            m = cur[1]
            for c in cur[2:]:
                m = jnp.maximum(m, c)
            tau = jnp.maximum(jnp.max(cur[0], axis=0, keepdims=True), jnp.max(m, axis=0, keepdims=True))
            cur = [jnp.where(c == tau, NEG_INF, c) for c in cur]
        a1 = jnp.exp(v1m - v1[0])
        a1h = jnp.exp(v1h - v1[0])
        z = jnp.sum(jnp.where(cands[0] >= tau, a1, 0.0), axis=0, keepdims=True)
        zsum = jnp.zeros_like(a1h)
        for jj in range(1, PEER_TOPK):
            zsum = zsum + jnp.where(cands[jj] >= tau, a1h * jnp.exp(v2[jj] - v2[0]), 0.0)
        z = z + jnp.sum(zsum, axis=0, keepdims=True)
        th = jnp.full(s1.shape, POS_INF, F32)
        for jj in range(PEER_TOPK):
            th = jnp.minimum(th, jnp.where(s1 + v2[jj] >= tau, v2[jj], POS_INF))
        a_o[h] = jnp.exp(s1 - v1[0]) / z
        th_o[h] = th
        b_o[h] = jnp.exp(s2 - v2[0])
        s2_o[h] = s2


def _pscore_call(h2, wqt, sk, ts):
    T, D = h2.shape
    out = jax.ShapeDtypeStruct((PEER_HEADS, N_KEYS, T), F32)
    ospec = pl.BlockSpec((PEER_HEADS, N_KEYS, ts), lambda i: (0, 0, i))
    return pl.pallas_call(
        _pscore_kernel,
        grid=(T // ts,),
        in_specs=[pl.BlockSpec((ts, D), lambda i: (i, 0)), _resident(wqt.shape), _resident(sk.shape)],
        out_specs=[ospec] * 4,
        out_shape=[out] * 4,
        compiler_params=_params(("parallel",)),
        name="pscore",
    )(h2, wqt, sk)


GATE_ROWS = 32
MXU_TILE = 256


def _dependent_zero(x):
    bits = pltpu.bitcast(x[0:SUBLANES, :], jnp.uint32)
    zero = lax.shift_right_logical(lax.shift_right_logical(bits, jnp.uint32(16)), jnp.uint32(16))
    return pltpu.bitcast(zero, BF16)


def _gate_tile(s2_ref, b_ref, a_ref, th_ref, n1b, g_scr, c, r):
    lanes = slice(c * LANES, (c + 1) * LANES)
    rows = slice(r * GATE_ROWS, (r + 1) * GATE_ROWS)
    accs = [None] * n1b
    for h in range(PEER_HEADS):
        s2t = s2_ref[h, rows, lanes]
        bt = b_ref[h, rows, lanes]
        for k in range(n1b):
            c_hk = jnp.where(s2t >= th_ref[h, k:k + 1, lanes], a_ref[h, k:k + 1, lanes] * bt, 0.0)
            accs[k] = c_hk if accs[k] is None else accs[k] + c_hk
    for k in range(n1b):
        g_scr[k * N_KEYS + r * GATE_ROWS:k * N_KEYS + (r + 1) * GATE_ROWS, lanes] = accs[k]
    return accs[n1b - 1]


def _pdense_kernel(h2_ref, u_ref, vt_ref, a_ref, th_ref, b_ref, s2_ref, x1_ref, gt_ref, gf_ref,
                   y_ref, h2t, acc, act_scr, g_scr, w_scr, *, n1b):
    i = pl.program_id(0)
    e = pl.program_id(1)
    d, tt = acc.shape

    @pl.when(e == 0)
    def _():
        h2t[...] = h2_ref[...].astype(F32).T.astype(BF16)
        acc[...] = jnp.zeros_like(acc)

    @pl.when(e >= 0)
    def _():
        tiles = [(c, r) for c in range(tt // LANES) for r in range(N_KEYS // GATE_ROWS)]
        nkt = d // MXU_TILE
        assert len(tiles) % nkt == 0
        per = len(tiles) // nkt
        act = None
        for k in range(nkt):
            for (c, r) in tiles[k * per:(k + 1) * per]:
                last = _gate_tile(s2_ref, b_ref, a_ref, th_ref, n1b, g_scr, c, r)
            wk = h2t[k * MXU_TILE:(k + 1) * MXU_TILE, :]
            top = wk[0:16, :]
            top = jnp.concatenate([top[:, 0:LANES] + _dependent_zero(last), top[:, LANES:]], axis=1)
            wk = jnp.concatenate([top, wk[16:]], axis=0)
            part = _dot(u_ref[:, k * MXU_TILE:(k + 1) * MXU_TILE], wk)
            act = part if act is None else act + part
        act_scr[...] = act

    @pl.when(i >= 0)
    def _():
        act = act_scr[...]
        gel = 0.5 * act * (1.0 + lax.erf(act * INV_SQRT2))
        w_scr[...] = (gel * g_scr[...]).astype(BF16)
        acc[...] += _dot(vt_ref[...], w_scr[...])

    @pl.when(e == pl.num_programs(1) - 1)
    def _():
        y = x1_ref[...] + gt_ref[...] * acc[...].T
        y_ref[...] = _rms(y, gf_ref[...])


def _pdense_call(h2, u, vt, a, th, b, s2, x1, gt, g_final, tt, n1b):
    T, D = h2.shape
    eb = n1b * N_KEYS
    nblk = u.shape[0] // eb
    tpg = T // tt // gt.shape[0]
    if gt.shape[1] == 1:
        gt_spec = pl.BlockSpec((None, 1, D), lambda i, e: (i // tpg, 0, 0))
    else:
        gt_spec = pl.BlockSpec((None, tt, D), lambda i, e: (i // tpg, i % tpg, 0))
    a = a.reshape(PEER_HEADS, N_KEYS // n1b, n1b, T)
    th = th.reshape(PEER_HEADS, N_KEYS // n1b, n1b, T)
    sel = pl.BlockSpec((PEER_HEADS, None, n1b, tt), lambda i, e: (0, e, 0, i))
    full = pl.BlockSpec((PEER_HEADS, N_KEYS, tt), lambda i, e: (0, 0, i))
    tok = pl.BlockSpec((tt, D), lambda i, e: (i, 0))
    return pl.pallas_call(
        functools.partial(_pdense_kernel, n1b=n1b),
        grid=(T // tt, nblk),
        in_specs=[tok, pl.BlockSpec((eb, D), lambda i, e: (e, 0)), pl.BlockSpec((D, eb), lambda i, e: (0, e)),
                  sel, sel, full, full, tok, gt_spec, _resident((1, D))],
        out_specs=tok,
        out_shape=jax.ShapeDtypeStruct((T, D), F32),
        scratch_shapes=[pltpu.VMEM((D, tt), BF16), pltpu.VMEM((D, tt), F32), pltpu.VMEM((eb, tt), F32),
                        pltpu.VMEM((eb, tt), F32), pltpu.VMEM((eb, tt), BF16)],
        compiler_params=_params(("parallel", "arbitrary")),
        name="pdense",
    )(h2, u, vt, a, th, b, s2, x1, gt, g_final)


def _tile(n, pref):
    t = min(n, pref)
    assert n % t == 0, (n, pref)
    return t


def _layer(x, mods, past, lw, g_final, batch, final):
    G, L, D = x.shape
    T = G * L
    sh1, sc1, gt1, sh2, sc2, gt2 = mods
    tm = _tile(L, 256)
    q, k, v, z, xbc, dt = _inproj_call(x, sh1, sc1, lw["g_mix"], lw["w_in"], tm)
    seq = T // batch
    bs = lambda t: t.reshape(batch, seq, t.shape[-1])
    q, k, v, z, xbc, dt = [bs(t) for t in (q, k, v, z, xbc, dt)]
    if past is None:
        att = _attn_call(q, k, v, None, None, lw["sinks"], lw["g_att"], _tile(seq, 256), True)
        tail0 = jnp.zeros((batch, SUBLANES, CONV_DIM), F32)
        h0t = jnp.zeros((batch, SSD_HEADS, SSD_STATE, SSD_HEADDIM), F32)
        lc = _tile(seq, 128)
        new_k, new_v = k[:, -WINDOW:], v[:, -WINDOW:]
    else:
        cache_k, cache_v, conv_prev, h0 = past
        att = _attn_call(q, k, v, cache_k.reshape(batch, WINDOW, KV_WIDTH), cache_v.reshape(batch, WINDOW, KV_WIDTH),
                         lw["sinks"], lw["g_att"], seq, False)
        tail0 = jnp.pad(conv_prev, ((0, 0), (SUBLANES - (CONV_W - 1), 0), (0, 0)))
        h0t = jnp.swapaxes(h0, -1, -2)
        lc = seq
        new_k, new_v = k, v
    y_ssd, tail, ht = _ssd_call(xbc, z, dt, tail0, h0t, lw["conv_w"], lw["conv_b"], lw["dt_bias"], lw["a_log"],
                                lw["d_skip"], lw["g_ssd"], lc)
    gs = lambda t: t.reshape(G, L, t.shape[-1])
    x1, h2 = _outproj_call(gs(att), gs(y_ssd), x, gt1, sh2, sc2, lw["g_ffn"], lw["w_out_a"], lw["w_out_y"], tm)
    h2f = h2.reshape(T, D)
    ts = _tile(T, 256)
    a, th, b, s2 = _pscore_call(h2f, lw["wq_t"], lw["sub_keys"], ts)
    tt = _tile(T, 512)
    gf = g_final if final else jnp.ones_like(g_final)
    y = _pdense_call(h2f, lw["peer_u"], lw["peer_vt"], a, th, b, s2, x1.reshape(T, D), gt2, gf, tt, 4)
    assert final
    new_k = new_k.reshape(batch, -1, ATT_KV_HEADS, HEAD_DIM)
    new_v = new_v.reshape(batch, -1, ATT_KV_HEADS, HEAD_DIM)
    conv_state = tail[:, SUBLANES - (CONV_W - 1):, :]
    ssm = jnp.swapaxes(ht, -1, -2)
    return y.reshape(G, L, D), (new_k, new_v, conv_state, ssm)


def _layer_weights(l, w_in, g_mix, attn_sinks, g_attn_out, conv_w, conv_b, dt_bias, a_log, d_skip, g_ssd, w_out,
                   g_ffn, w_peer_q, peer_sub_keys, peer_u, peer_v):
    d = w_in.shape[1]
    wb = w_in[l].astype(BF16)
    q_end = ATT_WIDTH
    k_end = q_end + KV_WIDTH
    v_end = k_end + KV_WIDTH
    z_end = v_end + SSD_WIDTH
    x_end = z_end + CONV_DIM
    wdt = jnp.pad(wb[:, x_end:], ((0, 0), (0, LANES - SSD_HEADS)))
    pad_h = lambda t: jnp.pad(t[l].reshape(1, SSD_HEADS), ((0, 0), (0, LANES - SSD_HEADS)))
    wo = w_out[l].astype(BF16)
    return {
        "w_in": (wb[:, :q_end], wb[:, q_end:k_end], wb[:, k_end:v_end], wb[:, v_end:z_end], wb[:, z_end:x_end], wdt),
        "g_mix": g_mix[l].reshape(1, d),
        "sinks": attn_sinks[l],
        "g_att": g_attn_out[l].reshape(1, ATT_WIDTH),
        "conv_w": conv_w[l],
        "conv_b": conv_b[l].reshape(1, CONV_DIM),
        "dt_bias": pad_h(dt_bias),
        "a_log": pad_h(a_log),
        "d_skip": d_skip[l],
        "g_ssd": g_ssd[l].reshape(1, SSD_WIDTH),
        "w_out_a": wo[:ATT_WIDTH],
        "w_out_y": wo[ATT_WIDTH:],
        "g_ffn": g_ffn[l].reshape(1, d),
        "wq_t": w_peer_q[l].astype(BF16).T,
        "sub_keys": peer_sub_keys[l].astype(BF16),
        "peer_u": peer_u[l].astype(BF16),
        "peer_vt": peer_v[l].astype(BF16).T,
    }


def kernel(x_prompt, x_sample, c_prompt, c_sample, cache_k, cache_v, state_conv, state_ssm, w_ada, b_ada, g_mix, w_in, attn_sinks, g_attn_out, conv_w, conv_b, dt_bias, a_log, d_skip, g_ssd, w_out, g_ffn, w_peer_q, peer_sub_keys, peer_u, peer_v, g_final):
    depth = w_ada.shape[0]
    assert depth == 1, "one layer per step"
    B, L, D = x_prompt.shape
    Bs, Ls, _ = x_sample.shape
    rows = B + Bs
    rows_pad = -(-rows // 16) * 16
    c_all = jnp.pad(jnp.concatenate([c_prompt, c_sample], axis=0), ((0, rows_pad - rows), (0, 0)))
    gfin = g_final.reshape(1, D)
    hp, hs = x_prompt, x_sample.reshape(1, Bs * Ls, D)
    outs_p, outs_s = [], []
    for l in range(depth):
        lw = _layer_weights(l, w_in, g_mix, attn_sinks, g_attn_out, conv_w, conv_b, dt_bias, a_log, d_skip, g_ssd,
                            w_out, g_ffn, w_peer_q, peer_sub_keys, peer_u, peer_v)
        mod = _mod_call(c_all, w_ada[l], b_ada[l]).reshape(rows_pad, 6, D)
        mods_p = [mod[:B, m].reshape(B, 1, D) for m in range(6)]
        mods_s = [jnp.broadcast_to(mod[B:rows, m][:, None, :], (Bs, Ls, D)).reshape(1, Bs * Ls, D) for m in range(6)]
        final = l == depth - 1
        hp, st_p = _layer(hp, mods_p, None, lw, gfin, B, final)
        hs, st_s = _layer(hs, mods_s, (cache_k[l], cache_v[l], state_conv[l], state_ssm[l]), lw, gfin, Bs, final)
        outs_p.append(st_p)
        outs_s.append(st_s)
    stack = lambda outs, i: jnp.stack([o[i] for o in outs])
    return (hp, hs.reshape(Bs, Ls, D),
            stack(outs_p, 0), stack(outs_p, 1), stack(outs_p, 2), stack(outs_p, 3),
            stack(outs_s, 0), stack(outs_s, 1), stack(outs_s, 2), stack(outs_s, 3))
```

```python
import functools
import math

import jax
import jax.numpy as jnp
from jax import lax
from jax.experimental import pallas as pl
from jax.experimental.pallas import tpu as pltpu

F32 = jnp.float32
BF16 = jnp.bfloat16

EPS = 1e-6
CHUNK = 64
WINDOW = 128
ATT_HEADS = 16
ATT_KV_HEADS = 4
ATT_GROUP = ATT_HEADS // ATT_KV_HEADS
HEAD_DIM = 64
ATT_WIDTH = ATT_HEADS * HEAD_DIM
KV_WIDTH = ATT_KV_HEADS * HEAD_DIM
ATT_SCALE = HEAD_DIM ** -0.5
SSD_HEADS = 16
SSD_HEADDIM = 64
SSD_WIDTH = SSD_HEADS * SSD_HEADDIM
SSD_GROUPS = 4
SSD_HPG = SSD_HEADS // SSD_GROUPS
SSD_STATE = 128
CONV_W = 4
CONV_DIM = SSD_WIDTH + 2 * SSD_GROUPS * SSD_STATE
PEER_HEADS = 8
N_KEYS = 128
D_HALF = 128
PEER_TOPK = 16

LANES = 128
SUBLANES = 8
VMEM_LIMIT = 56 * 1024 * 1024
INV_SQRT2 = 0.7071067811865476
NEG_INF = float("-inf")
POS_INF = float("inf")


def _params(sem):
    return pltpu.CompilerParams(dimension_semantics=sem, vmem_limit_bytes=VMEM_LIMIT)


def _resident(shape):
    nd = len(shape)
    return pl.BlockSpec(shape, lambda *_: (0,) * nd, pipeline_mode=pl.Buffered(1))


def _rms(x, g):
    return x * lax.rsqrt(jnp.mean(x * x, axis=-1, keepdims=True) + EPS) * g


def _silu(x):
    return x * jax.nn.sigmoid(x)


def _softplus(x):
    return jnp.maximum(x, 0.0) + jnp.log1p(jnp.exp(-jnp.abs(x)))


def _dot(a, b):
    return jnp.dot(a, b, preferred_element_type=F32)


def _dot_nt(a, b):
    return lax.dot_general(a, b, (((1,), (1,)), ((), ())), preferred_element_type=F32)


def _mod_kernel(c_ref, w_ref, b_ref, o_ref):
    s = _silu(c_ref[...]).astype(BF16)
    o_ref[...] = _dot(s, w_ref[...].astype(BF16)) + b_ref[...]


def _mod_call(c_all, w_ada, b_ada):
    rows, d = c_all.shape
    n = w_ada.shape[1]
    bn = 1536
    assert n % bn == 0
    return pl.pallas_call(
        _mod_kernel,
        grid=(n // bn,),
        in_specs=[pl.BlockSpec((rows, d), lambda j: (0, 0)),
                  pl.BlockSpec((d, bn), lambda j: (0, j)),
                  pl.BlockSpec((1, bn), lambda j: (0, j))],
        out_specs=pl.BlockSpec((rows, bn), lambda j: (0, j)),
        out_shape=jax.ShapeDtypeStruct((rows, n), F32),
        compiler_params=_params(("arbitrary",)),
        name="mod",
    )(c_all, w_ada, b_ada.reshape(1, n))


def _inproj_kernel(x_ref, sh_ref, sc_ref, g_ref, wq, wk, wv, wz, wx, wdt,
                   q_o, k_o, v_o, z_o, xbc_o, dt_o):
    h = _rms(x_ref[...], g_ref[...]) * (1.0 + sc_ref[...]) + sh_ref[...]
    hb = h.astype(BF16)
    q_o[...] = _dot(hb, wq[...]).astype(q_o.dtype)
    k_o[...] = _dot(hb, wk[...])
    v_o[...] = _dot(hb, wv[...])
    z_o[...] = _dot(hb, wz[...])
    xbc_o[...] = _dot(hb, wx[...])
    dt_o[...] = _dot(hb, wdt[...])


def _mod_spec(arr, tm):
    r = arr.shape[1]
    d = arr.shape[2]
    if r == 1:
        return pl.BlockSpec((None, 1, d), lambda b, i: (b, 0, 0))
    return pl.BlockSpec((None, tm, d), lambda b, i: (b, i, 0))


def _inproj_call(x, sh, sc, g_mix, ws, tm):
    G, L, D = x.shape
    wq, wk, wv, wz, wx, wdt = ws
    tok = lambda n: pl.BlockSpec((None, tm, n), lambda b, i: (b, i, 0))
    out_w = (ATT_WIDTH, KV_WIDTH, KV_WIDTH, SSD_WIDTH, CONV_DIM, LANES)
    out_dt = (BF16, F32, F32, F32, F32, F32)
    return pl.pallas_call(
        _inproj_kernel,
        grid=(G, L // tm),
        in_specs=[tok(D), _mod_spec(sh, tm), _mod_spec(sc, tm), _resident((1, D))]
                 + [_resident(w.shape) for w in ws],
        out_specs=[tok(n) for n in out_w],
        out_shape=[jax.ShapeDtypeStruct((G, L, n), dt) for n, dt in zip(out_w, out_dt)],
        compiler_params=_params(("parallel", "arbitrary")),
        name="inproj",
    )(x, sh, sc, g_mix, *ws)


def _alibi_slope(h):
    return 2.0 ** (-8.0 * (h + 1) / ATT_HEADS)


def _attn_kernel(sink_ref, q_ref, kp_ref, kc_ref, vp_ref, vc_ref, g_ref, o_ref, *, banded, qb):
    blk = pl.program_id(1)
    nk = WINDOW + qb
    r = lax.broadcasted_iota(jnp.int32, (qb, nk), 0)
    j = lax.broadcasted_iota(jnp.int32, (qb, nk), 1)
    dist = jnp.abs(r + WINDOW - j).astype(F32)
    if banded:
        qc = jnp.right_shift(r, 6)
        kc = jnp.right_shift(j, 6) - WINDOW // CHUNK
        valid = (kc <= qc) & (kc >= qc - WINDOW // CHUNK) & ((j >= WINDOW) | (blk > 0))
    q = q_ref[...]
    outs = []
    for kv in range(ATT_KV_HEADS):
        sl = slice(kv * HEAD_DIM, (kv + 1) * HEAD_DIM)
        kb = jnp.concatenate([kp_ref[:, sl], kc_ref[:, sl]], axis=0).astype(BF16)
        vb = jnp.concatenate([vp_ref[:, sl], vc_ref[:, sl]], axis=0).astype(BF16)
        for g in range(ATT_GROUP):
            h = kv * ATT_GROUP + g
            qh = q[:, h * HEAD_DIM:(h + 1) * HEAD_DIM]
            logits = _dot_nt(qh, kb) * ATT_SCALE - _alibi_slope(h) * dist
            if banded:
                logits = jnp.where(valid, logits, NEG_INF)
            sink = sink_ref[h]
            m = jnp.maximum(jnp.max(logits, axis=-1, keepdims=True), sink)
            p = jnp.exp(logits - m)
            den = jnp.sum(p, axis=-1, keepdims=True) + jnp.exp(sink - m)
            outs.append(_dot(p.astype(BF16), vb) / den)
    att = jnp.concatenate(outs, axis=-1)
    o_ref[...] = _rms(att, g_ref[...]).astype(o_ref.dtype)


def _attn_call(q, k, v, k_prev, v_prev, sinks, g_att, qb, banded):
    B, L, _ = q.shape
    nb = L // qb
    wpb = qb // WINDOW
    cur = lambda n: pl.BlockSpec((None, qb, n), lambda b, i: (b, i, 0))
    if k_prev is None:
        prev = pl.BlockSpec((None, WINDOW, KV_WIDTH), lambda b, i: (b, jnp.maximum(i * wpb - 1, 0), 0))
        k_prev, v_prev = k, v
    else:
        prev = pl.BlockSpec((None, WINDOW, KV_WIDTH), lambda b, i: (b, 0, 0))
    return pl.pallas_call(
        functools.partial(_attn_kernel, banded=banded, qb=qb),
        grid=(B, nb),
        in_specs=[pl.BlockSpec(memory_space=pltpu.SMEM),
                  cur(ATT_WIDTH), prev, cur(KV_WIDTH), prev, cur(KV_WIDTH), _resident((1, ATT_WIDTH))],
        out_specs=cur(ATT_WIDTH),
        out_shape=jax.ShapeDtypeStruct((B, L, ATT_WIDTH), BF16),
        compiler_params=_params(("parallel", "arbitrary")),
        name="attn",
    )(sinks, q, k_prev, k, v_prev, v, g_att)


def _ssd_kernel(dskip_ref, xbc_ref, z_ref, dt_ref, tail0_ref, h0_ref, cw_ref, cb_ref, dtb_ref, alog_ref, g_ref,
                y_ref, tail_ref, hT_ref, xext, hT, *, lc):
    i = pl.program_id(1)

    @pl.when(i == 0)
    def _():
        xext[0:SUBLANES, :] = tail0_ref[...]
        hT[...] = h0_ref[...]

    xext[SUBLANES:SUBLANES + lc, :] = xbc_ref[...]
    acc = cb_ref[...]
    for t in range(CONV_W):
        off = SUBLANES - (CONV_W - 1) + t
        acc = acc + xext[off:off + lc, :] * cw_ref[t:t + 1, :]
    xa = _silu(acc)
    tail = xext[lc:lc + SUBLANES, :]
    xext[0:SUBLANES, :] = tail
    tail_ref[...] = tail

    xs = xa[:, :SSD_WIDTH]
    bm = xa[:, SSD_WIDTH:SSD_WIDTH + SSD_GROUPS * SSD_STATE]
    cm = xa[:, SSD_WIDTH + SSD_GROUPS * SSD_STATE:]
    dt = _softplus(dt_ref[...] + dtb_ref[...])
    da = dt * (-jnp.exp(alog_ref[...]))
    row = lax.broadcasted_iota(jnp.int32, (lc, lc), 0)
    col = lax.broadcasted_iota(jnp.int32, (lc, lc), 1)
    causal = row >= col
    acum = jnp.dot(causal.astype(F32), da, precision=lax.Precision.HIGHEST, preferred_element_type=F32)
    acum_t = acum.T
    a_last = acum[lc - 1:lc, :]
    dec_end = jnp.exp(a_last - acum)
    eac = jnp.exp(acum)
    cdec = jnp.exp(a_last)
    ys = []
    for g in range(SSD_GROUPS):
        bg = bm[:, g * SSD_STATE:(g + 1) * SSD_STATE]
        cgb = cm[:, g * SSD_STATE:(g + 1) * SSD_STATE].astype(BF16)
        cb = _dot_nt(cgb, bg.astype(BF16))
        bgt = bg.T.astype(BF16)
        for r in range(SSD_HPG):
            h = g * SSD_HPG + r
            seg = acum[:, h:h + 1] - acum_t[h:h + 1, :]
            decay = jnp.exp(jnp.where(causal, seg, NEG_INF))
            xh = xs[:, h * SSD_HEADDIM:(h + 1) * SSD_HEADDIM]
            xdt = xh * dt[:, h:h + 1]
            hprev = hT[h]
            y = _dot((cb * decay).astype(BF16), xdt.astype(BF16))
            y = y + _dot(cgb, hprev.astype(BF16)) * eac[:, h:h + 1]
            st = _dot(bgt, (xdt * dec_end[:, h:h + 1]).astype(BF16))
            hT[h] = hprev * cdec[:, h:h + 1] + st
            ys.append(y + dskip_ref[h] * xh)
    y = jnp.concatenate(ys, axis=-1) * _silu(z_ref[...])
    y_ref[...] = _rms(y, g_ref[...]).astype(y_ref.dtype)
    hT_ref[...] = hT[...]


def _ssd_call(xbc, z, dt, tail0, h0t, conv_w, conv_b, dt_bias, a_log, d_skip, g_ssd, lc):
    B, L, _ = xbc.shape
    tok = lambda n: pl.BlockSpec((None, lc, n), lambda b, i: (b, i, 0))
    per_b = lambda shp: pl.BlockSpec((None,) + shp, lambda b, i: (b,) + (0,) * len(shp))
    return pl.pallas_call(
        functools.partial(_ssd_kernel, lc=lc),
        grid=(B, L // lc),
        in_specs=[pl.BlockSpec(memory_space=pltpu.SMEM),
                  tok(CONV_DIM), tok(SSD_WIDTH), tok(LANES),
                  per_b((SUBLANES, CONV_DIM)), per_b((SSD_HEADS, SSD_STATE, SSD_HEADDIM)),
                  _resident((CONV_W, CONV_DIM)), _resident((1, CONV_DIM)),
                  _resident((1, LANES)), _resident((1, LANES)), _resident((1, SSD_WIDTH))],
        out_specs=[tok(SSD_WIDTH), per_b((SUBLANES, CONV_DIM)), per_b((SSD_HEADS, SSD_STATE, SSD_HEADDIM))],
        out_shape=[jax.ShapeDtypeStruct((B, L, SSD_WIDTH), BF16),
                   jax.ShapeDtypeStruct((B, SUBLANES, CONV_DIM), F32),
                   jax.ShapeDtypeStruct((B, SSD_HEADS, SSD_STATE, SSD_HEADDIM), F32)],
        scratch_shapes=[pltpu.VMEM((lc + SUBLANES, CONV_DIM), F32),
                        pltpu.VMEM((SSD_HEADS, SSD_STATE, SSD_HEADDIM), F32)],
        compiler_params=_params(("parallel", "arbitrary")),
        name="ssd",
    )(d_skip, xbc, z, dt, tail0, h0t, conv_w, conv_b, dt_bias, a_log, g_ssd)


def _outproj_kernel(att_ref, y_ref, x_ref, gt_ref, sh_ref, sc_ref, g_ref, wa, wy, x1_o, h2_o):
    mixed = _dot(att_ref[...], wa[...]) + _dot(y_ref[...], wy[...])
    x1 = x_ref[...] + gt_ref[...] * mixed
    x1_o[...] = x1
    h2_o[...] = (_rms(x1, g_ref[...]) * (1.0 + sc_ref[...]) + sh_ref[...]).astype(h2_o.dtype)


def _outproj_call(att, y, x, gt, sh, sc, g_ffn, wa, wy, tm):
    G, L, D = x.shape
    tok = lambda n: pl.BlockSpec((None, tm, n), lambda b, i: (b, i, 0))
    return pl.pallas_call(
        _outproj_kernel,
        grid=(G, L // tm),
        in_specs=[tok(ATT_WIDTH), tok(SSD_WIDTH), tok(D), _mod_spec(gt, tm), _mod_spec(sh, tm), _mod_spec(sc, tm),
                  _resident((1, D)), _resident(wa.shape), _resident(wy.shape)],
        out_specs=[tok(D), tok(D)],
        out_shape=[jax.ShapeDtypeStruct((G, L, D), F32), jax.ShapeDtypeStruct((G, L, D), BF16)],
        compiler_params=_params(("parallel", "arbitrary")),
        name="outproj",
    )(att, y, x, gt, sh, sc, g_ffn, wa, wy)


def _top_rows(s, k):
    rows = []
    cur = s
    for _ in range(k):
        m = jnp.max(cur, axis=0, keepdims=True)
        rows.append(m)
        cur = jnp.where(cur == m, NEG_INF, cur)
    return rows


def _pscore_kernel(h2_ref, wqt_ref, sk_ref, a_o, th_o, b_o, s2_o):
    qt = _dot_nt(wqt_ref[...], h2_ref[...])
    ts = qt.shape[1]
    sub = lax.broadcasted_iota(jnp.int32, (PEER_TOPK, ts), 0)
    for h in range(PEER_HEADS):
        base = h * 2 * D_HALF
        s1 = _dot(sk_ref[h, 0], qt[base:base + D_HALF].astype(BF16))
        s2 = _dot(sk_ref[h, 1], qt[base + D_HALF:base + 2 * D_HALF].astype(BF16))
        v1 = _top_rows(s1, PEER_TOPK)
        v2 = _top_rows(s2, PEER_TOPK)
        v1m = jnp.concatenate(v1, axis=0)
        cands = [jnp.where(sub < PEER_TOPK // (jj + 1), v1m + v2[jj], NEG_INF) for jj in range(PEER_TOPK)]
        cur = cands
        tau = None
        for _ in range(PEER_TOPK):
            m = cur[0]
            for c in cur[1:]:
                m = jnp.maximum(m, c)
            tau = jnp.max(m, axis=0, keepdims=True)
            cur = [jnp.where(c == tau, NEG_INF, c) for c in cur]
        a1 = jnp.exp(v1m - v1[0])
        zsum = jnp.zeros_like(a1)
        for jj in range(PEER_TOPK):
            zsum = zsum + jnp.where(cands[jj] >= tau, a1 * jnp.exp(v2[jj] - v2[0]), 0.0)
        z = jnp.sum(zsum, axis=0, keepdims=True)
        th = jnp.full(s1.shape, POS_INF, F32)
        for jj in range(PEER_TOPK):
            th = jnp.minimum(th, jnp.where(s1 + v2[jj] >= tau, v2[jj], POS_INF))
        a_o[h] = jnp.exp(s1 - v1[0]) / z
        th_o[h] = th
        b_o[h] = jnp.exp(s2 - v2[0])
        s2_o[h] = s2


def _pscore_call(h2, wqt, sk, ts):
    T, D = h2.shape
    out = jax.ShapeDtypeStruct((PEER_HEADS, N_KEYS, T), F32)
    ospec = pl.BlockSpec((PEER_HEADS, N_KEYS, ts), lambda i: (0, 0, i))
    return pl.pallas_call(
        _pscore_kernel,
        grid=(T // ts,),
        in_specs=[pl.BlockSpec((ts, D), lambda i: (i, 0)), _resident(wqt.shape), _resident(sk.shape)],
        out_specs=[ospec] * 4,
        out_shape=[out] * 4,
        compiler_params=_params(("parallel",)),
        name="pscore",
    )(h2, wqt, sk)


GATE_ROWS = 32
MXU_TILE = 256


def _dependent_zero(x):
    bits = pltpu.bitcast(x[0:SUBLANES, :], jnp.uint32)
    zero = lax.shift_right_logical(lax.shift_right_logical(bits, jnp.uint32(16)), jnp.uint32(16))
    return pltpu.bitcast(zero, BF16)


def _gate_tile(s2_ref, b_ref, a_ref, th_ref, n1b, g_scr, c, r):
    lanes = slice(c * LANES, (c + 1) * LANES)
    rows = slice(r * GATE_ROWS, (r + 1) * GATE_ROWS)
    accs = [None] * n1b
    for h in range(PEER_HEADS):
        s2t = s2_ref[h, rows, lanes]
        bt = b_ref[h, rows, lanes]
        for k in range(n1b):
            c_hk = jnp.where(s2t >= th_ref[h, k:k + 1, lanes], a_ref[h, k:k + 1, lanes] * bt, 0.0)
            accs[k] = c_hk if accs[k] is None else accs[k] + c_hk
    for k in range(n1b):
        g_scr[k * N_KEYS + r * GATE_ROWS:k * N_KEYS + (r + 1) * GATE_ROWS, lanes] = accs[k]
    return accs[n1b - 1]


def _pdense_kernel(h2_ref, u_ref, vt_ref, a_ref, th_ref, b_ref, s2_ref, x1_ref, gt_ref, gf_ref,
                   y_ref, h2t, acc, act_scr, g_scr, w_scr, *, n1b):
    i = pl.program_id(0)
    e = pl.program_id(1)
    d, tt = acc.shape

    @pl.when(e == 0)
    def _():
        h2t[...] = h2_ref[...].astype(F32).T.astype(BF16)
        acc[...] = jnp.zeros_like(acc)

    @pl.when(e >= 0)
    def _():
        tiles = [(c, r) for c in range(tt // LANES) for r in range(N_KEYS // GATE_ROWS)]
        nkt = d // MXU_TILE
        assert len(tiles) % nkt == 0
        per = len(tiles) // nkt
        act = None
        for k in range(nkt):
            for (c, r) in tiles[k * per:(k + 1) * per]:
                last = _gate_tile(s2_ref, b_ref, a_ref, th_ref, n1b, g_scr, c, r)
            wk = h2t[k * MXU_TILE:(k + 1) * MXU_TILE, :]
            top = wk[0:16, :]
            top = jnp.concatenate([top[:, 0:LANES] + _dependent_zero(last), top[:, LANES:]], axis=1)
            wk = jnp.concatenate([top, wk[16:]], axis=0)
            part = _dot(u_ref[:, k * MXU_TILE:(k + 1) * MXU_TILE], wk)
            act = part if act is None else act + part
        act_scr[...] = act

    @pl.when(i >= 0)
    def _():
        act = act_scr[...]
        gel = 0.5 * act * (1.0 + lax.erf(act * INV_SQRT2))
        w_scr[...] = (gel * g_scr[...]).astype(BF16)
        acc[...] += _dot(vt_ref[...], w_scr[...])

    @pl.when(e == pl.num_programs(1) - 1)
    def _():
        y = x1_ref[...] + gt_ref[...] * acc[...].T
        y_ref[...] = _rms(y, gf_ref[...])


def _pdense_call(h2, u, vt, a, th, b, s2, x1, gt, g_final, tt, n1b):
    T, D = h2.shape
    eb = n1b * N_KEYS
    nblk = u.shape[0] // eb
    tpg = T // tt // gt.shape[0]
    if gt.shape[1] == 1:
        gt_spec = pl.BlockSpec((None, 1, D), lambda i, e: (i // tpg, 0, 0))
    else:
        gt_spec = pl.BlockSpec((None, tt, D), lambda i, e: (i // tpg, i % tpg, 0))
    a = a.reshape(PEER_HEADS, N_KEYS // n1b, n1b, T)
    th = th.reshape(PEER_HEADS, N_KEYS // n1b, n1b, T)
    sel = pl.BlockSpec((PEER_HEADS, None, n1b, tt), lambda i, e: (0, e, 0, i))
    full = pl.BlockSpec((PEER_HEADS, N_KEYS, tt), lambda i, e: (0, 0, i))
    tok = pl.BlockSpec((tt, D), lambda i, e: (i, 0))
    return pl.pallas_call(
        functools.partial(_pdense_kernel, n1b=n1b),
        grid=(T // tt, nblk),
        in_specs=[tok, pl.BlockSpec((eb, D), lambda i, e: (e, 0)), pl.BlockSpec((D, eb), lambda i, e: (0, e)),
                  sel, sel, full, full, tok, gt_spec, _resident((1, D))],
        out_specs=tok,
        out_shape=jax.ShapeDtypeStruct((T, D), F32),
        scratch_shapes=[pltpu.VMEM((D, tt), BF16), pltpu.VMEM((D, tt), F32), pltpu.VMEM((eb, tt), F32),
                        pltpu.VMEM((eb, tt), F32), pltpu.VMEM((eb, tt), BF16)],
        compiler_params=_params(("parallel", "arbitrary")),
        name="pdense",
    )(h2, u, vt, a, th, b, s2, x1, gt, g_final)


def _tile(n, pref):
    t = min(n, pref)
    assert n % t == 0, (n, pref)
    return t


def _layer(x, mods, past, lw, g_final, batch, final):
    G, L, D = x.shape
    T = G * L
    sh1, sc1, gt1, sh2, sc2, gt2 = mods
    tm = _tile(L, 256)
    q, k, v, z, xbc, dt = _inproj_call(x, sh1, sc1, lw["g_mix"], lw["w_in"], tm)
    seq = T // batch
    bs = lambda t: t.reshape(batch, seq, t.shape[-1])
    q, k, v, z, xbc, dt = [bs(t) for t in (q, k, v, z, xbc, dt)]
    if past is None:
        att = _attn_call(q, k, v, None, None, lw["sinks"], lw["g_att"], _tile(seq, 256), True)
        tail0 = jnp.zeros((batch, SUBLANES, CONV_DIM), F32)
        h0t = jnp.zeros((batch, SSD_HEADS, SSD_STATE, SSD_HEADDIM), F32)
        lc = _tile(seq, 256)
        new_k, new_v = k[:, -WINDOW:], v[:, -WINDOW:]
    else:
        cache_k, cache_v, conv_prev, h0 = past
        att = _attn_call(q, k, v, cache_k.reshape(batch, WINDOW, KV_WIDTH), cache_v.reshape(batch, WINDOW, KV_WIDTH),
                         lw["sinks"], lw["g_att"], seq, False)
        tail0 = jnp.pad(conv_prev, ((0, 0), (SUBLANES - (CONV_W - 1), 0), (0, 0)))
        h0t = jnp.swapaxes(h0, -1, -2)
        lc = seq
        new_k, new_v = k, v
    y_ssd, tail, ht = _ssd_call(xbc, z, dt, tail0, h0t, lw["conv_w"], lw["conv_b"], lw["dt_bias"], lw["a_log"],
                                lw["d_skip"], lw["g_ssd"], lc)
    gs = lambda t: t.reshape(G, L, t.shape[-1])
    x1, h2 = _outproj_call(gs(att), gs(y_ssd), x, gt1, sh2, sc2, lw["g_ffn"], lw["w_out_a"], lw["w_out_y"], tm)
    h2f = h2.reshape(T, D)
    ts = _tile(T, 256)
    a, th, b, s2 = _pscore_call(h2f, lw["wq_t"], lw["sub_keys"], ts)
    tt = _tile(T, 512)
    gf = g_final if final else jnp.ones_like(g_final)
    y = _pdense_call(h2f, lw["peer_u"], lw["peer_vt"], a, th, b, s2, x1.reshape(T, D), gt2, gf, tt, 4)
    assert final
    new_k = new_k.reshape(batch, -1, ATT_KV_HEADS, HEAD_DIM)
    new_v = new_v.reshape(batch, -1, ATT_KV_HEADS, HEAD_DIM)
    conv_state = tail[:, SUBLANES - (CONV_W - 1):, :]
    ssm = jnp.swapaxes(ht, -1, -2)
    return y.reshape(G, L, D), (new_k, new_v, conv_state, ssm)


def _layer_weights(l, w_in, g_mix, attn_sinks, g_attn_out, conv_w, conv_b, dt_bias, a_log, d_skip, g_ssd, w_out,
                   g_ffn, w_peer_q, peer_sub_keys, peer_u, peer_v):
    d = w_in.shape[1]
    wb = w_in[l].astype(BF16)
    q_end = ATT_WIDTH
    k_end = q_end + KV_WIDTH
    v_end = k_end + KV_WIDTH
    z_end = v_end + SSD_WIDTH
    x_end = z_end + CONV_DIM
    wdt = jnp.pad(wb[:, x_end:], ((0, 0), (0, LANES - SSD_HEADS)))
    pad_h = lambda t: jnp.pad(t[l].reshape(1, SSD_HEADS), ((0, 0), (0, LANES - SSD_HEADS)))
    wo = w_out[l].astype(BF16)
    return {
        "w_in": (wb[:, :q_end], wb[:, q_end:k_end], wb[:, k_end:v_end], wb[:, v_end:z_end], wb[:, z_end:x_end], wdt),
        "g_mix": g_mix[l].reshape(1, d),
        "sinks": attn_sinks[l],
        "g_att": g_attn_out[l].reshape(1, ATT_WIDTH),
        "conv_w": conv_w[l],
        "conv_b": conv_b[l].reshape(1, CONV_DIM),
        "dt_bias": pad_h(dt_bias),
        "a_log": pad_h(a_log),
        "d_skip": d_skip[l],
        "g_ssd": g_ssd[l].reshape(1, SSD_WIDTH),
        "w_out_a": wo[:ATT_WIDTH],
        "w_out_y": wo[ATT_WIDTH:],
        "g_ffn": g_ffn[l].reshape(1, d),
        "wq_t": w_peer_q[l].astype(BF16).T,
        "sub_keys": peer_sub_keys[l].astype(BF16),
        "peer_u": peer_u[l].astype(BF16),
        "peer_vt": peer_v[l].astype(BF16).T,
    }


def kernel(x_prompt, x_sample, c_prompt, c_sample, cache_k, cache_v, state_conv, state_ssm, w_ada, b_ada, g_mix, w_in, attn_sinks, g_attn_out, conv_w, conv_b, dt_bias, a_log, d_skip, g_ssd, w_out, g_ffn, w_peer_q, peer_sub_keys, peer_u, peer_v, g_final):
    depth = w_ada.shape[0]
    assert depth == 1, "one layer per step"
    B, L, D = x_prompt.shape
    Bs, Ls, _ = x_sample.shape
    rows = B + Bs
    rows_pad = -(-rows // 16) * 16
    c_all = jnp.pad(jnp.concatenate([c_prompt, c_sample], axis=0), ((0, rows_pad - rows), (0, 0)))
    gfin = g_final.reshape(1, D)
    hp, hs = x_prompt, x_sample.reshape(1, Bs * Ls, D)
    outs_p, outs_s = [], []
    for l in range(depth):
        lw = _layer_weights(l, w_in, g_mix, attn_sinks, g_attn_out, conv_w, conv_b, dt_bias, a_log, d_skip, g_ssd,
                            w_out, g_ffn, w_peer_q, peer_sub_keys, peer_u, peer_v)
        mod = _mod_call(c_all, w_ada[l], b_ada[l]).reshape(rows_pad, 6, D)
        mods_p = [mod[:B, m].reshape(B, 1, D) for m in range(6)]
        mods_s = [jnp.broadcast_to(mod[B:rows, m][:, None, :], (Bs, Ls, D)).reshape(1, Bs * Ls, D) for m in range(6)]
        final = l == depth - 1
        hp, st_p = _layer(hp, mods_p, None, lw, gfin, B, final)
        hs, st_s = _layer(hs, mods_s, (cache_k[l], cache_v[l], state_conv[l], state_ssm[l]), lw, gfin, Bs, final)
        outs_p.append(st_p)
        outs_s.append(st_s)
    stack = lambda outs, i: jnp.stack([o[i] for o in outs])
    return (hp, hs.reshape(Bs, Ls, D),
            stack(outs_p, 0), stack(outs_p, 1), stack(outs_p, 2), stack(outs_p, 3),
            stack(outs_s, 0), stack(outs_s, 1), stack(outs_s, 2), stack(outs_s, 3))
```
